```python
import math
import jax, jax.numpy as jnp
from jax import lax
import numpy as np

D_MODEL = 1024
BATCH = 2
SEQ = 16384
DEPTH = 2

GRID_W = 64
CTX_LEN = 256
N_MIXERS = 2
N_HEADS = 16
HEAD_DIM = D_MODEL // N_HEADS
WIN_H = 8
WIN_W = 16
Q_BLOCK = 128
D_RNN = 1536
RNN_BLOCKS = 16
RNN_BW = D_RNN // RNN_BLOCKS
CONV_W = 4
CONV_LEFT = 2
LRU_C = 8.0
D_FF = 2816
N_EXPERTS = 8
TOP_K = 2
D_FF_EXPERT = 3584
EPS = 1e-6

kernel_name = "hybrid_na_rglru_moe_dit"


def rmsnorm(x, g):
    xf = x.astype(jnp.float32)
    y = xf * lax.rsqrt(jnp.mean(xf * xf, axis=-1, keepdims=True) + EPS)
    return (y * g.astype(jnp.float32)).astype(x.dtype)


def adaln(cvec, w_mod, b_mod):
    m = jax.nn.silu(cvec) @ w_mod + b_mod
    return jnp.split(m, 6, axis=-1)


def modulate(h, shift, scale):
    return h * (1.0 + scale) + shift


def swiglu(h, w1, w3, w2):
    return (jax.nn.silu(h @ w1) * (h @ w3)) @ w2


def moe_ffn(h, w_router, b_router, w1, w3, w2):
    logits = (h @ w_router).astype(jnp.float32) + b_router.astype(jnp.float32)
    top_v, top_i = lax.top_k(logits, TOP_K)
    gates = jax.nn.softmax(top_v, axis=-1)
    comb = jnp.sum(jax.nn.one_hot(top_i, N_EXPERTS, dtype=jnp.float32) * gates[..., None], axis=-2)
    comb = comb.astype(h.dtype)
    y = jnp.zeros_like(h)
    for e in range(N_EXPERTS):
        y = y + comb[..., e:e + 1] * swiglu(h, w1[e], w3[e], w2[e])
    return y


def na_tables(rows):
    kh = min(WIN_H, rows)
    s = rows * GRID_W
    t = jnp.arange(s, dtype=jnp.int32)
    r = t // GRID_W
    col = t % GRID_W
    rs = jnp.clip(r - kh // 2, 0, rows - kh)
    cs = jnp.clip(col - WIN_W // 2, 0, GRID_W - WIN_W)
    kr = rs[:, None, None] + jnp.arange(kh, dtype=jnp.int32)[None, :, None]
    kc = cs[:, None, None] + jnp.arange(WIN_W, dtype=jnp.int32)[None, None, :]
    idx = (kr * GRID_W + kc).reshape(s, kh * WIN_W)
    dr = kr - r[:, None, None] + (WIN_H - 1)
    dc = kc - col[:, None, None] + (WIN_W - 1)
    bidx = (dr * (2 * WIN_W - 1) + dc).reshape(s, kh * WIN_W)
    return idx, bidx


def na_mixer(h_lat, h_ctx, w_qkv, q_gain, k_gain, rpb, w_o, need_ctx):
    b, s, d = h_lat.shape
    rows = s // GRID_W
    scale = HEAD_DIM ** -0.5

    def qkv(h):
        q, k, v = jnp.split(h @ w_qkv, 3, axis=-1)
        shp = h.shape[:2] + (N_HEADS, HEAD_DIM)
        return rmsnorm(q.reshape(shp), q_gain), rmsnorm(k.reshape(shp), k_gain), v.reshape(shp)

    ql, kl, vl = qkv(h_lat)
    qc, kc, vc = qkv(h_ctx)
    idx, bidx = na_tables(rows)
    n_keys = idx.shape[1]
    bias_tab = rpb.reshape(N_HEADS, -1)
    nb = s // Q_BLOCK

    def block(args):
        q_b, idx_b, bidx_b = args
        k_g = kl[:, idx_b]
        v_g = vl[:, idx_b]
        s_loc = jnp.einsum('bqhd,bqkhd->bhqk', q_b, k_g) * scale + bias_tab[:, bidx_b][None]
        s_ctx = jnp.einsum('bqhd,bchd->bhqc', q_b, kc) * scale
        p = jax.nn.softmax(jnp.concatenate([s_loc, s_ctx], axis=-1).astype(jnp.float32), axis=-1)
        p = p.astype(q_b.dtype)
        return (jnp.einsum('bhqk,bqkhd->bqhd', p[..., :n_keys], v_g)
                + jnp.einsum('bhqc,bchd->bqhd', p[..., n_keys:], vc))

    q_blocks = ql.reshape(b, nb, Q_BLOCK, N_HEADS, HEAD_DIM).transpose(1, 0, 2, 3, 4)
    o = lax.map(block, (q_blocks, idx.reshape(nb, Q_BLOCK, n_keys), bidx.reshape(nb, Q_BLOCK, n_keys)))
    o = o.transpose(1, 0, 2, 3, 4).reshape(b, s, d)
    y_lat = o @ w_o
    y_ctx = None
    if need_ctx:
        sc = jnp.einsum('bqhd,bkhd->bhqk', qc, kc) * scale
        pc = jax.nn.softmax(sc.astype(jnp.float32), axis=-1).astype(qc.dtype)
        oc = jnp.einsum('bhqk,bkhd->bqhd', pc, vc)
        y_ctx = oc.reshape(h_ctx.shape[0], h_ctx.shape[1], d) @ w_o
    return y_lat, y_ctx


def dwconv(x, w, bias):
    y = lax.conv_general_dilated(x, w[:, None, :], window_strides=(1,),
                                 padding=[(CONV_LEFT, CONV_W - 1 - CONV_LEFT)],
                                 dimension_numbers=('NWC', 'WIO', 'NWC'),
                                 feature_group_count=x.shape[-1])
    return y + bias


def blockdiag(x, w, bias):
    xb = x.reshape(x.shape[:-1] + (RNN_BLOCKS, RNN_BW))
    return jnp.einsum('bnkc,kcd->bnkd', xb, w).reshape(x.shape) + bias


def lru_coeffs(xc, wa, ba, wx, bx, lam):
    xf = xc.astype(jnp.float32)
    r = jax.nn.sigmoid(blockdiag(xc, wa, ba).astype(jnp.float32))
    i = jax.nn.sigmoid(blockdiag(xc, wx, bx).astype(jnp.float32))
    log_a = -LRU_C * r * jax.nn.softplus(-lam.astype(jnp.float32))
    a = jnp.exp(log_a)
    mult = jnp.sqrt(-jnp.expm1(2.0 * log_a))
    return a, mult * (i * xf)


def linear_scan(a, u, h0, reverse):
    def comb(lft, rgt):
        return (lft[0] * rgt[0], rgt[0] * lft[1] + rgt[1])
    a_cum, b_cum = lax.associative_scan(comb, (a, u), reverse=reverse, axis=1)
    return a_cum * h0[:, None, :] + b_cum


def rglru_mixer(h_lat, h_ctx, w_in, conv_w, conv_b, gate_a_w, gate_a_b, gate_x_w, gate_x_b, lam,
                w_out, need_ctx):
    def branches(h):
        g, xr = jnp.split(h @ w_in, 2, axis=-1)
        return g, dwconv(xr, conv_w, conv_b)

    g_l, xc_l = branches(h_lat)
    g_c, xc_c = branches(h_ctx)
    zeros0 = jnp.zeros((h_ctx.shape[0], D_RNN), jnp.float32)
    hs_l = jnp.zeros(xc_l.shape, jnp.float32)
    hs_c = jnp.zeros(xc_c.shape, jnp.float32)
    for d, rev in enumerate((False, True)):
        a_c, u_c = lru_coeffs(xc_c, gate_a_w[d], gate_a_b[d], gate_x_w[d], gate_x_b[d], lam[d])
        h_c = linear_scan(a_c, u_c, zeros0, rev)
        h0 = h_c[:, 0] if rev else h_c[:, -1]
        a_l, u_l = lru_coeffs(xc_l, gate_a_w[d], gate_a_b[d], gate_x_w[d], gate_x_b[d], lam[d])
        hs_l = hs_l + linear_scan(a_l, u_l, h0, rev)
        hs_c = hs_c + h_c
    y_lat = (jax.nn.gelu(g_l) * hs_l.astype(g_l.dtype)) @ w_out
    y_ctx = None
    if need_ctx:
        y_ctx = (jax.nn.gelu(g_c) * hs_c.astype(g_c.dtype)) @ w_out
    return y_lat, y_ctx


def setup_inputs(seed: int = 0) -> dict:
    key = jax.random.key(seed)
    ks = list(jax.random.split(key, 40))
    nrm = lambda k, shp, s: jax.random.normal(k, shp, jnp.float32) * s
    D = D_MODEL
    inp = {}
    inp['x'] = nrm(ks[0], (BATCH, SEQ, D), 1.0)
    inp['c'] = nrm(ks[1], (BATCH, D), 1.0)
    inp['ctx'] = nrm(ks[2], (BATCH, CTX_LEN, D), 1.0)
    inp['c_ctx'] = nrm(ks[3], (D,), 1.0)
    inp['l0_w_mod'] = nrm(ks[4], (D, 6 * D), 0.5 * D ** -0.5)
    inp['l0_b_mod'] = nrm(ks[5], (6 * D,), 0.02)
    inp['l0_norm1'] = 1.0 + nrm(ks[6], (D,), 0.05)
    inp['l0_norm2'] = 1.0 + nrm(ks[7], (D,), 0.05)
    inp['l0_w_qkv'] = nrm(ks[8], (D, 3 * D), D ** -0.5)
    inp['l0_q_gain'] = 1.0 + nrm(ks[9], (HEAD_DIM,), 0.05)
    inp['l0_k_gain'] = 1.0 + nrm(ks[10], (HEAD_DIM,), 0.05)
    inp['l0_rpb'] = nrm(ks[11], (N_HEADS, 2 * WIN_H - 1, 2 * WIN_W - 1), 0.2)
    inp['l0_w_o'] = nrm(ks[12], (D, D), D ** -0.5)
    inp['l0_ffn_w1'] = nrm(ks[13], (D, D_FF), D ** -0.5)
    inp['l0_ffn_w3'] = nrm(ks[14], (D, D_FF), D ** -0.5)
    inp['l0_ffn_w2'] = nrm(ks[15], (D_FF, D), D_FF ** -0.5)
    inp['l1_w_mod'] = nrm(ks[16], (D, 6 * D), 0.5 * D ** -0.5)
    inp['l1_b_mod'] = nrm(ks[17], (6 * D,), 0.02)
    inp['l1_norm1'] = 1.0 + nrm(ks[18], (D,), 0.05)
    inp['l1_norm2'] = 1.0 + nrm(ks[19], (D,), 0.05)
    inp['l1_w_in'] = nrm(ks[20], (D, 2 * D_RNN), D ** -0.5)
    inp['l1_conv_w'] = nrm(ks[21], (CONV_W, D_RNN), CONV_W ** -0.5)
    inp['l1_conv_b'] = nrm(ks[22], (D_RNN,), 0.02)
    inp['l1_gate_a_w'] = nrm(ks[23], (2, RNN_BLOCKS, RNN_BW, RNN_BW), RNN_BW ** -0.5)
    inp['l1_gate_a_b'] = nrm(ks[24], (2, D_RNN), 0.02)
    inp['l1_gate_x_w'] = nrm(ks[25], (2, RNN_BLOCKS, RNN_BW, RNN_BW), RNN_BW ** -0.5)
    inp['l1_gate_x_b'] = nrm(ks[26], (2, D_RNN), 0.02)
    u = jax.random.uniform(ks[27], (2, D_RNN), jnp.float32, minval=0.9, maxval=0.999)
    s = u ** (1.0 / LRU_C)
    inp['l1_lam'] = jnp.log(s) - jnp.log1p(-s)
    inp['l1_w_out'] = nrm(ks[28], (D_RNN, D), D_RNN ** -0.5)
    inp['l1_router_w'] = nrm(ks[29], (D, N_EXPERTS), D ** -0.5)
    inp['l1_router_b'] = nrm(ks[30], (N_EXPERTS,), 0.01)
    inp['l1_moe_w1'] = nrm(ks[31], (N_EXPERTS, D, D_FF_EXPERT), D ** -0.5)
    inp['l1_moe_w3'] = nrm(ks[32], (N_EXPERTS, D, D_FF_EXPERT), D ** -0.5)
    inp['l1_moe_w2'] = nrm(ks[33], (N_EXPERTS, D_FF_EXPERT, D), D_FF_EXPERT ** -0.5)
    return inp


def reference(x, c, ctx, c_ctx,
              l0_w_mod, l0_b_mod, l0_norm1, l0_norm2, l0_w_qkv, l0_q_gain, l0_k_gain, l0_rpb, l0_w_o,
              l0_ffn_w1, l0_ffn_w3, l0_ffn_w2,
              l1_w_mod, l1_b_mod, l1_norm1, l1_norm2, l1_w_in, l1_conv_w, l1_conv_b,
              l1_gate_a_w, l1_gate_a_b, l1_gate_x_w, l1_gate_x_b, l1_lam, l1_w_out,
              l1_router_w, l1_router_b, l1_moe_w1, l1_moe_w3, l1_moe_w2):
    layers = [
        dict(w_mod=l0_w_mod, b_mod=l0_b_mod, norm1=l0_norm1, norm2=l0_norm2,
             mixer=dict(w_qkv=l0_w_qkv, q_gain=l0_q_gain, k_gain=l0_k_gain, rpb=l0_rpb, w_o=l0_w_o),
             ffn=dict(w1=l0_ffn_w1, w3=l0_ffn_w3, w2=l0_ffn_w2)),
        dict(w_mod=l1_w_mod, b_mod=l1_b_mod, norm1=l1_norm1, norm2=l1_norm2,
             mixer=dict(w_in=l1_w_in, conv_w=l1_conv_w, conv_b=l1_conv_b, gate_a_w=l1_gate_a_w,
                        gate_a_b=l1_gate_a_b, gate_x_w=l1_gate_x_w, gate_x_b=l1_gate_x_b,
                        lam=l1_lam, w_out=l1_w_out),
             ffn=dict(w_router=l1_router_w, b_router=l1_router_b, w1=l1_moe_w1, w3=l1_moe_w3,
                      w2=l1_moe_w2)),
    ]
    x_lat, x_ctx = x, ctx
    for i in range(DEPTH):
        p = layers[i]
        need_ctx = i < DEPTH - 1
        m_l = adaln(c[:, None, :], p['w_mod'], p['b_mod'])
        m_c = adaln(c_ctx[None, None, :], p['w_mod'], p['b_mod'])
        h_l = modulate(rmsnorm(x_lat, p['norm1']), m_l[0], m_l[1])
        h_c = modulate(rmsnorm(x_ctx, p['norm1']), m_c[0], m_c[1])
        if i % N_MIXERS == 0:
            y_l, y_c = na_mixer(h_l, h_c, need_ctx=need_ctx, **p['mixer'])
        else:
            y_l, y_c = rglru_mixer(h_l, h_c, need_ctx=need_ctx, **p['mixer'])
        x_lat = x_lat + m_l[2] * y_l
        h_l = modulate(rmsnorm(x_lat, p['norm2']), m_l[3], m_l[4])
        ffn = swiglu if i % 2 == 0 else moe_ffn
        f_args = p['ffn']
        if i % 2 == 0:
            f_l = swiglu(h_l, f_args['w1'], f_args['w3'], f_args['w2'])
        else:
            f_l = moe_ffn(h_l, f_args['w_router'], f_args['b_router'], f_args['w1'], f_args['w3'], f_args['w2'])
        x_lat = x_lat + m_l[5] * f_l
        if need_ctx:
            x_ctx = x_ctx + m_c[2] * y_c
            h_c = modulate(rmsnorm(x_ctx, p['norm2']), m_c[3], m_c[4])
            if i % 2 == 0:
                f_c = swiglu(h_c, f_args['w1'], f_args['w3'], f_args['w2'])
            else:
                f_c = moe_ffn(h_c, f_args['w_router'], f_args['b_router'], f_args['w1'], f_args['w3'], f_args['w2'])
            x_ctx = x_ctx + m_c[5] * f_c
    return x_lat
```

```python
import functools

import numpy as np
import jax
import jax.numpy as jnp
from jax import lax
from jax.experimental import pallas as pl
from jax.experimental.pallas import tpu as pltpu

GRID_W = 64
WIN_H = 8
WIN_W = 16
TOP_K = 2
LRU_C = 8.0
EPS = 1e-6
CONV_LEFT = 2
NEG_BIG = -1e30
LANES = 128
SUBLANES = 8
VMEM_LIMIT = 56 * 1024 * 1024

ATT_ROWS = 8
ATT_KROWS = 16
ATT_PAIR = 2
ATT_BAND = 10

F32 = jnp.float32
BF16 = jnp.bfloat16


def _cparams(sem):
    return pltpu.CompilerParams(dimension_semantics=sem, vmem_limit_bytes=VMEM_LIMIT)


def _rms_mod(x, g, shift, scale):
    ms = jnp.mean(x * x, axis=-1, keepdims=True)
    y = x * lax.rsqrt(ms + EPS) * g
    return y * (1.0 + scale) + shift


def _mod_kernel(ct_ref, w_ref, b_ref, o_ref, *, nrows):
    ct = ct_ref[...]
    s = ct * jax.nn.sigmoid(ct)
    w = w_ref[...]
    rows = [jnp.sum(w * s[:, m:m + 1], axis=0, keepdims=True) + b_ref[...] for m in range(nrows)]
    rows += [jnp.zeros_like(rows[0])] * (SUBLANES - nrows)
    o_ref[...] = jnp.concatenate(rows, axis=0)


def _adaln_mod(cvecs, w_mod, b_mod):
    nrows, d = cvecs.shape
    n = w_mod.shape[1]
    tn = 768
    ct = jnp.zeros((d, SUBLANES), F32).at[:, :nrows].set(cvecs.T)
    return pl.pallas_call(
        functools.partial(_mod_kernel, nrows=nrows),
        grid=(n // tn,),
        in_specs=[pl.BlockSpec((d, SUBLANES), lambda j: (0, 0)),
                  pl.BlockSpec((d, tn), lambda j: (0, j)),
                  pl.BlockSpec((1, tn), lambda j: (0, j))],
        out_specs=pl.BlockSpec((SUBLANES, tn), lambda j: (0, j)),
        out_shape=jax.ShapeDtypeStruct((SUBLANES, n), F32),
        compiler_params=_cparams(("arbitrary",)),
        name="adaln_mod",
    )(ct, w_mod, b_mod.reshape(1, n))


def _qkv_kernel(x_ref, g_ref, sh_ref, sc_ref, w_ref, qg_ref, kg_ref, pm_ref, pe_ref,
                q_ref, k_ref, v_ref, *, d):
    h = _rms_mod(x_ref[0], g_ref[...], sh_ref[0], sc_ref[0]).astype(BF16)
    for part, (gain_ref, out_ref) in enumerate(((qg_ref, q_ref), (kg_ref, k_ref))):
        y = jnp.dot(h, w_ref[:, part * d:(part + 1) * d], preferred_element_type=F32)
        ms = jnp.dot((y * y).astype(BF16), pm_ref[...], preferred_element_type=F32)
        inv = lax.rsqrt(ms + EPS)
        inv_hi = inv.astype(BF16)
        inv_lo = (inv - inv_hi.astype(F32)).astype(BF16)
        full = (jnp.dot(inv_hi, pe_ref[...], preferred_element_type=F32)
                + jnp.dot(inv_lo, pe_ref[...], preferred_element_type=F32))
        out_ref[0] = (y * full * gain_ref[...]).astype(BF16)
    v_ref[0] = jnp.dot(h, w_ref[:, 2 * d:3 * d], preferred_element_type=F32).astype(BF16)


def _qkv(x, norm_g, shift, scale, w_qkv_bf, q_gain_full, k_gain_full, head_dim, tm):
    bx, sx, d = x.shape
    n_heads = d // head_dim
    head_of = np.arange(d) // head_dim
    pm = np.zeros((d, LANES), np.float32)
    pm[np.arange(d), head_of] = 1.0 / head_dim
    pe = np.zeros((LANES, d), np.float32)
    pe[head_of, np.arange(d)] = 1.0
    assert n_heads <= LANES
    row = lambda b, i: (b, i, 0)
    vec = lambda b, i: (b, 0, 0)
    const2 = lambda b, i: (0, 0)
    out = jax.ShapeDtypeStruct((bx, sx, d), BF16)
    return pl.pallas_call(
        functools.partial(_qkv_kernel, d=d),
        grid=(bx, sx // tm),
        in_specs=[pl.BlockSpec((1, tm, d), row),
                  pl.BlockSpec((1, d), const2),
                  pl.BlockSpec((1, 1, d), vec),
                  pl.BlockSpec((1, 1, d), vec),
                  pl.BlockSpec((d, 3 * d), const2),
                  pl.BlockSpec((1, d), const2),
                  pl.BlockSpec((1, d), const2),
                  pl.BlockSpec((d, LANES), const2),
                  pl.BlockSpec((LANES, d), const2)],
        out_specs=[pl.BlockSpec((1, tm, d), row)] * 3,
        out_shape=[out, out, out],
        compiler_params=_cparams(("parallel", "parallel")),
        name="qkv_headnorm",
    )(x, norm_g.reshape(1, d), shift, scale, w_qkv_bf, q_gain_full, k_gain_full,
      jnp.asarray(pm, BF16), jnp.asarray(pe, BF16))


def _att_band_layout(rows):
    npair = ATT_ROWS // ATT_PAIR
    half_win = WIN_H // 2
    sets = []
    for kind, off in (("mid", half_win), ("top", 0), ("bot", ATT_KROWS - ATT_ROWS)):
        for p in range(npair):
            if kind == "mid" and p > 0:
                continue
            rs_rel = []
            for a in range(ATT_PAIR):
                rq = ATT_PAIR * p + a
                if kind == "mid":
                    rs_rel.append(rq)
                elif kind == "top":
                    rs_rel.append(max(rq - half_win, 0))
                else:
                    rs_rel.append(min(rq + off - half_win, ATT_KROWS - WIN_H))
            m0 = min(rs_rel[0] // 2, (ATT_KROWS - ATT_BAND) // 2)
            sets.append((kind, p, off, rs_rel, m0))
    return sets


def _att_bias_table(rpb):
    n_heads = rpb.shape[0]
    sets = _att_band_layout(None)
    nq = ATT_PAIR * GRID_W
    nk = ATT_BAND * GRID_W
    dr_idx = np.zeros((len(sets), nq, nk), np.int32)
    dc_idx = np.zeros((len(sets), nq, nk), np.int32)
    valid = np.zeros((len(sets), nq, nk), bool)
    cq = np.arange(GRID_W)
    cs = np.clip(cq - WIN_W // 2, 0, GRID_W - WIN_W)
    ck = np.arange(GRID_W)
    col_ok = (ck[None, :] >= cs[:, None]) & (ck[None, :] < cs[:, None] + WIN_W)
    dc = ck[None, :] - cq[:, None] + (WIN_W - 1)
    for si, (kind, p, off, rs_rel, m0) in enumerate(sets):
        for a in range(ATT_PAIR):
            rq = ATT_PAIR * p + a if kind != "mid" else a
            rs = rs_rel[a] if kind != "mid" else a
            base = m0 if kind != "mid" else 0
            for j in range(ATT_BAND):
                rk = 2 * base + j
                row_ok = rs <= rk < rs + WIN_H
                dr = rk - rq - off + (WIN_H - 1)
                qs = slice(a * GRID_W, (a + 1) * GRID_W)
                ks = slice(j * GRID_W, (j + 1) * GRID_W)
                ok = col_ok & row_ok
                valid[si, qs, ks] = ok
                dr_idx[si, qs, ks] = np.where(ok, dr, 0)
                dc_idx[si, qs, ks] = np.where(ok, dc, 0)
    vals = rpb[:, dr_idx, dc_idx]
    return jnp.where(jnp.asarray(valid)[None], vals, NEG_BIG).astype(F32)


def _attn_kernel(q_ref, k0, k1, k2, k3, v0, v1, v2, v3, kc_ref, vc_ref, tb_ref, o_ref,
                 kwin, vwin, *, m0_top, m0_bot):
    i = pl.program_id(2)
    nt = pl.num_programs(2)
    ck = k0.shape[2]
    for c, (kr, vr) in enumerate(((k0, v0), (k1, v1), (k2, v2), (k3, v3))):
        kwin[c * ck:(c + 1) * ck, :] = kr[0, 0]
        vwin[c * ck:(c + 1) * ck, :] = vr[0, 0]
    kc = kc_ref[0]
    vc = vc_ref[0]
    lane = lax.broadcasted_iota(jnp.int32, (1, LANES), 1)
    head_lo = lane < (LANES // 2)
    nq = ATT_PAIR * GRID_W
    nk = ATT_BAND * GRID_W
    npair = ATT_ROWS // ATT_PAIR
    dn = (((1,), (1,)), ((), ()))
    for p in range(npair):
        m0 = jnp.where(i == 0, m0_top[p], jnp.where(i == nt - 1, m0_bot[p], p))
        st = jnp.where(i == 0, 1 + p, jnp.where(i == nt - 1, 1 + npair + p, 0))
        start = pl.multiple_of(m0 * LANES, LANES)
        kb = kwin[pl.ds(start, nk), :]
        vb = vwin[pl.ds(start, nk), :]
        qp = q_ref[0, p * nq:(p + 1) * nq, :]
        outs = []
        for hh in range(2):
            mask = head_lo if hh == 0 else jnp.logical_not(head_lo)
            qh = jnp.where(mask, qp, jnp.zeros_like(qp))
            s = lax.dot_general(qh, kb, dn, preferred_element_type=F32) + tb_ref[hh, st]
            sc = lax.dot_general(qh, kc, dn, preferred_element_type=F32)
            m = jnp.maximum(jnp.max(s, axis=-1, keepdims=True), jnp.max(sc, axis=-1, keepdims=True))
            e = jnp.exp(s - m)
            ec = jnp.exp(sc - m)
            den = jnp.sum(e, axis=-1, keepdims=True) + jnp.sum(ec, axis=-1, keepdims=True)
            o = (jnp.dot(e.astype(BF16), vb, preferred_element_type=F32)
                 + jnp.dot(ec.astype(BF16), vc, preferred_element_type=F32))
            outs.append(o * (1.0 / den))
        o_ref[0, p * nq:(p + 1) * nq, :] = jnp.where(head_lo, outs[0], outs[1]).astype(BF16)


def _attention(q, k, v, kc, vc, bias_tab):
    b, s, d = q.shape
    c = kc.shape[1]
    rows = s // GRID_W
    assert rows % ATT_ROWS == 0 and rows >= ATT_KROWS
    tq = ATT_ROWS * GRID_W
    nt = s // tq
    chunk_rows = 4
    ck = chunk_rows * GRID_W
    nchunk = ATT_KROWS // chunk_rows
    k4 = k.reshape(b, s // ck, ck, d)
    v4 = v.reshape(b, s // ck, ck, d)
    sets = _att_band_layout(None)
    npair = ATT_ROWS // ATT_PAIR
    m0_top = tuple(sets[1 + p][4] for p in range(npair))
    m0_bot = tuple(sets[1 + npair + p][4] for p in range(npair))

    def kmap(t):
        def f(bi, j, i):
            c0 = jnp.clip(i * (ATT_ROWS // chunk_rows) - (WIN_H // 2) // chunk_rows, 0,
                          (rows - ATT_KROWS) // chunk_rows)
            return (bi, c0 + t, 0, j)
        return f

    kv_specs = [pl.BlockSpec((1, 1, ck, LANES), kmap(t)) for t in range(nchunk)]
    n_sets = bias_tab.shape[1]
    return pl.pallas_call(
        functools.partial(_attn_kernel, m0_top=m0_top, m0_bot=m0_bot),
        grid=(b, d // LANES, nt),
        in_specs=[pl.BlockSpec((1, tq, LANES), lambda bi, j, i: (bi, i, j))]
                 + kv_specs + kv_specs
                 + [pl.BlockSpec((1, c, LANES), lambda bi, j, i: (bi, 0, j)),
                    pl.BlockSpec((1, c, LANES), lambda bi, j, i: (bi, 0, j)),
                    pl.BlockSpec((2, n_sets, ATT_PAIR * GRID_W, ATT_BAND * GRID_W),
                                 lambda bi, j, i: (j, 0, 0, 0))],
        out_specs=pl.BlockSpec((1, tq, LANES), lambda bi, j, i: (bi, i, j)),
        out_shape=jax.ShapeDtypeStruct((b, s, d), BF16),
        scratch_shapes=[pltpu.VMEM((ATT_KROWS * GRID_W, LANES), BF16),
                        pltpu.VMEM((ATT_KROWS * GRID_W, LANES), BF16)],
        compiler_params=_cparams(("parallel", "parallel", "arbitrary")),
        name="nbr_attention",
    )(q, *([k4] * nchunk), *([v4] * nchunk), kc, vc, bias_tab)


def _ctx_attn_kernel(q_ref, k_ref, v_ref, o_ref):
    lane = lax.broadcasted_iota(jnp.int32, (1, LANES), 1)
    head_lo = lane < (LANES // 2)
    q = q_ref[0]
    k = k_ref[0]
    v = v_ref[0]
    outs = []
    for hh in range(2):
        mask = head_lo if hh == 0 else jnp.logical_not(head_lo)
        qh = jnp.where(mask, q, jnp.zeros_like(q))
        s = lax.dot_general(qh, k, (((1,), (1,)), ((), ())), preferred_element_type=F32)
        m = jnp.max(s, axis=-1, keepdims=True)
        e = jnp.exp(s - m)
        den = jnp.sum(e, axis=-1, keepdims=True)
        o = jnp.dot(e.astype(BF16), v, preferred_element_type=F32)
        outs.append(o * (1.0 / den))
    o_ref[0] = jnp.where(head_lo, outs[0], outs[1]).astype(BF16)


def _ctx_attention(q, k, v):
    b, c, d = q.shape
    spec = pl.BlockSpec((1, c, LANES), lambda bi, j: (bi, 0, j))
    return pl.pallas_call(
        _ctx_attn_kernel,
        grid=(b, d // LANES),
        in_specs=[spec, spec, spec],
        out_specs=spec,
        out_shape=jax.ShapeDtypeStruct((b, c, d), BF16),
        compiler_params=_cparams(("parallel", "parallel")),
        name="ctx_attention",
    )(q, k, v)


def _proj_res_kernel(a_ref, w_ref, x_ref, gate_ref, o_ref):
    y = jnp.dot(a_ref[0], w_ref[...], preferred_element_type=F32)
    o_ref[0] = x_ref[0] + gate_ref[0] * y


def _proj_residual(a, w_bf, x, gate, tm):
    bx, sx, kdim = a.shape
    d = x.shape[-1]
    row = lambda b, i: (b, i, 0)
    return pl.pallas_call(
        _proj_res_kernel,
        grid=(bx, sx // tm),
        in_specs=[pl.BlockSpec((1, tm, kdim), row),
                  pl.BlockSpec((kdim, d), lambda b, i: (0, 0)),
                  pl.BlockSpec((1, tm, d), row),
                  pl.BlockSpec((1, 1, d), lambda b, i: (b, 0, 0))],
        out_specs=pl.BlockSpec((1, tm, d), row),
        out_shape=jax.ShapeDtypeStruct(x.shape, F32),
        compiler_params=_cparams(("parallel", "parallel")),
        name="proj_residual",
    )(a, w_bf, x, gate)


def _ffn_kernel(x_ref, g_ref, sh_ref, sc_ref, gate_ref, w1_ref, w3_ref, w2_ref, o_ref, h_scr, acc_scr):
    f = pl.program_id(2)

    @pl.when(f == 0)
    def _():
        h_scr[...] = _rms_mod(x_ref[0], g_ref[...], sh_ref[0], sc_ref[0]).astype(BF16)
        acc_scr[...] = jnp.zeros_like(acc_scr)

    h = h_scr[...]
    a = jnp.dot(h, w1_ref[...], preferred_element_type=F32)
    b = jnp.dot(h, w3_ref[...], preferred_element_type=F32)
    t = (a * jax.nn.sigmoid(a) * b).astype(BF16)
    acc_scr[...] += jnp.dot(t, w2_ref[...], preferred_element_type=F32)

    @pl.when(f == pl.num_programs(2) - 1)
    def _():
        o_ref[0] = x_ref[0] + gate_ref[0] * acc_scr[...]


def _ffn(x, norm_g, shift, scale, gate, w1_bf, w3_bf, w2_bf, tm, tf):
    bx, sx, d = x.shape
    ff = w1_bf.shape[1]
    row = lambda b, i, f: (b, i, 0)
    vec = lambda b, i, f: (b, 0, 0)
    return pl.pallas_call(
        _ffn_kernel,
        grid=(bx, sx // tm, ff // tf),
        in_specs=[pl.BlockSpec((1, tm, d), row),
                  pl.BlockSpec((1, d), lambda b, i, f: (0, 0)),
                  pl.BlockSpec((1, 1, d), vec),
                  pl.BlockSpec((1, 1, d), vec),
                  pl.BlockSpec((1, 1, d), vec),
                  pl.BlockSpec((d, tf), lambda b, i, f: (0, f)),
                  pl.BlockSpec((d, tf), lambda b, i, f: (0, f)),
                  pl.BlockSpec((tf, d), lambda b, i, f: (f, 0))],
        out_specs=pl.BlockSpec((1, tm, d), row),
        out_shape=jax.ShapeDtypeStruct(x.shape, F32),
        scratch_shapes=[pltpu.VMEM((tm, d), BF16), pltpu.VMEM((tm, d), F32)],
        compiler_params=_cparams(("parallel", "parallel", "arbitrary")),
        name="ffn_swiglu",
    )(x, norm_g.reshape(1, d), shift, scale, gate, w1_bf, w3_bf, w2_bf)


def _inproj_kernel(x_ref, g_ref, sh_ref, sc_ref, w_ref, gb_ref, xr_ref, *, dr):
    h = _rms_mod(x_ref[0], g_ref[...], sh_ref[0], sc_ref[0]).astype(BF16)
    gb_ref[0] = jnp.dot(h, w_ref[:, :dr], preferred_element_type=F32)
    xr_ref[0] = jnp.dot(h, w_ref[:, dr:], preferred_element_type=F32)


def _inproj(x, norm_g, shift, scale, w_in_bf, tm):
    bx, sx, d = x.shape
    dr = w_in_bf.shape[1] // 2
    row = lambda b, i: (b, i, 0)
    vec = lambda b, i: (b, 0, 0)
    out = jax.ShapeDtypeStruct((bx, sx, dr), F32)
    return pl.pallas_call(
        functools.partial(_inproj_kernel, dr=dr),
        grid=(bx, sx // tm),
        in_specs=[pl.BlockSpec((1, tm, d), row),
                  pl.BlockSpec((1, d), lambda b, i: (0, 0)),
                  pl.BlockSpec((1, 1, d), vec),
                  pl.BlockSpec((1, 1, d), vec),
                  pl.BlockSpec((d, 2 * dr), lambda b, i: (0, 0))],
        out_specs=[pl.BlockSpec((1, tm, dr), row)] * 2,
        out_shape=[out, out],
        compiler_params=_cparams(("parallel", "parallel")),
        name="rglru_inproj",
    )(x, norm_g.reshape(1, d), shift, scale, w_in_bf)


def _lru_scan_kernel(x_ref, xp_ref, xn_ref, cw_ref, cb_ref, wa_ref, wx_ref, ba_ref, bx_ref, lam_ref,
                     h0_ref, o_ref, ext, a_scr, u_scr, carry, *, reverse, ngroups):
    i = pl.program_id(1)
    nb = pl.num_programs(1)
    blk = (nb - 1 - i) if reverse else i
    t, c = x_ref.shape[1], x_ref.shape[2]
    pad = SUBLANES

    @pl.when(i == 0)
    def _():
        carry[...] = jnp.broadcast_to(h0_ref[0], carry.shape)

    ext[0:pad, :] = jnp.where(blk > 0, xp_ref[0], 0.0)
    ext[pad:pad + t, :] = x_ref[0]
    ext[pad + t:pad + t + pad, :] = jnp.where(blk < nb - 1, xn_ref[0], 0.0)
    cw = cw_ref[...]
    xc = cb_ref[...] + sum(cw[j:j + 1, :] * ext[pad + j - CONV_LEFT:pad + j - CONV_LEFT + t, :]
                           for j in range(cw.shape[0]))

    xb = xc.astype(BF16)
    gw = c // ngroups
    pre_a, pre_x = [], []
    for gi in range(ngroups):
        xg = xb[:, gi * gw:(gi + 1) * gw]
        pre_a.append(jnp.dot(xg, wa_ref[gi], preferred_element_type=F32))
        pre_x.append(jnp.dot(xg, wx_ref[gi], preferred_element_type=F32))
    r = jax.nn.sigmoid(jnp.concatenate(pre_a, axis=1) + ba_ref[...])
    gi_ = jax.nn.sigmoid(jnp.concatenate(pre_x, axis=1) + bx_ref[...])
    lam = lam_ref[...]
    sp = jnp.maximum(-lam, 0.0) + jnp.log(1.0 + jnp.exp(-jnp.abs(lam)))
    a = jnp.exp((-LRU_C) * r * sp)
    u = jnp.sqrt(1.0 - a * a) * (gi_ * xc)

    sub = lax.broadcasted_iota(jnp.int32, (t, 1), 0) % SUBLANES
    for sft in (1, 2, 4):
        if reverse:
            a_sh = pltpu.roll(a, t - sft, axis=0)
            u_sh = pltpu.roll(u, t - sft, axis=0)
            ok = sub < SUBLANES - sft
        else:
            a_sh = pltpu.roll(a, sft, axis=0)
            u_sh = pltpu.roll(u, sft, axis=0)
            ok = sub >= sft
        u = jnp.where(ok, a * u_sh + u, u)
        a = jnp.where(ok, a * a_sh, a)
    a_scr[...] = a
    u_scr[...] = u
    ng = t // SUBLANES
    edge = 0 if reverse else SUBLANES - 1

    def body(j, hprev):
        g = (ng - 1 - j) if reverse else j
        off = pl.multiple_of(g * SUBLANES, SUBLANES)
        hblk = a_scr[pl.ds(off, SUBLANES), :] * hprev + u_scr[pl.ds(off, SUBLANES), :]
        o_ref[0, pl.ds(off, SUBLANES), :] = hblk
        return jnp.broadcast_to(hblk[edge:edge + 1, :], hblk.shape)

    carry[...] = lax.fori_loop(0, ng, body, carry[...])


def _lru_scan(xr, conv_w, conv_b, wa_g, wx_g, ba, bx, lam, h0, reverse, t):
    b, s, c = xr.shape
    nb = s // t
    ngroups = wa_g.shape[0]
    gw = c // ngroups
    tb = t // SUBLANES
    if reverse:
        pos = lambda i: nb - 1 - i
    else:
        pos = lambda i: i
    const2 = lambda bi, i: (0, 0)
    const3 = lambda bi, i: (0, 0, 0)
    return pl.pallas_call(
        functools.partial(_lru_scan_kernel, reverse=reverse, ngroups=ngroups),
        grid=(b, nb),
        in_specs=[pl.BlockSpec((1, t, c), lambda bi, i: (bi, pos(i), 0)),
                  pl.BlockSpec((1, SUBLANES, c), lambda bi, i: (bi, jnp.maximum(pos(i) * tb - 1, 0), 0)),
                  pl.BlockSpec((1, SUBLANES, c),
                               lambda bi, i: (bi, jnp.minimum((pos(i) + 1) * tb, s // SUBLANES - 1), 0)),
                  pl.BlockSpec(conv_w.shape, const2),
                  pl.BlockSpec((1, c), const2),
                  pl.BlockSpec((ngroups, gw, gw), const3),
                  pl.BlockSpec((ngroups, gw, gw), const3),
                  pl.BlockSpec((1, c), const2),
                  pl.BlockSpec((1, c), const2),
                  pl.BlockSpec((1, c), const2),
                  pl.BlockSpec((1, 1, c), lambda bi, i: (bi, 0, 0))],
        out_specs=pl.BlockSpec((1, t, c), lambda bi, i: (bi, pos(i), 0)),
        out_shape=jax.ShapeDtypeStruct((b, s, c), F32),
        scratch_shapes=[pltpu.VMEM((t + 2 * SUBLANES, c), F32),
                        pltpu.VMEM((t, c), F32),
                        pltpu.VMEM((t, c), F32),
                        pltpu.VMEM((SUBLANES, c), F32)],
        compiler_params=_cparams(("parallel", "arbitrary")),
        name="lru_scan_bwd" if reverse else "lru_scan_fwd",
    )(xr, xr, xr, conv_w, conv_b.reshape(1, c), wa_g, wx_g, ba.reshape(1, c), bx.reshape(1, c),
      lam.reshape(1, c), h0)


def _lru_out_kernel(g_ref, hf_ref, hb_ref, w_ref, x_ref, gate_ref, o_ref):
    a = (jax.nn.gelu(g_ref[0]) * (hf_ref[0] + hb_ref[0])).astype(BF16)
    y = jnp.dot(a, w_ref[...], preferred_element_type=F32)
    o_ref[0] = x_ref[0] + gate_ref[0] * y


def _lru_out(g, hf, hb, w_out_bf, x, gate, tm):
    bx, sx, dr = g.shape
    d = x.shape[-1]
    row = lambda b, i: (b, i, 0)
    return pl.pallas_call(
        _lru_out_kernel,
        grid=(bx, sx // tm),
        in_specs=[pl.BlockSpec((1, tm, dr), row)] * 3
                 + [pl.BlockSpec((dr, d), lambda b, i: (0, 0)),
                    pl.BlockSpec((1, tm, d), row),
                    pl.BlockSpec((1, 1, d), lambda b, i: (b, 0, 0))],
        out_specs=pl.BlockSpec((1, tm, d), row),
        out_shape=jax.ShapeDtypeStruct(x.shape, F32),
        compiler_params=_cparams(("parallel", "parallel")),
        name="rglru_outproj",
    )(g, hf, hb, w_out_bf, x, gate)


def _router_kernel(x_ref, g_ref, sh_ref, sc_ref, wh_ref, wl_ref, br_ref, hp_ref, rt_ref, *, n_exp):
    h = _rms_mod(x_ref[0], g_ref[...], sh_ref[0], sc_ref[0])
    hb = h.astype(BF16)
    hf = hb.astype(F32)
    h_lo = (h - hf).astype(BF16)
    logits = (jnp.dot(hb, wh_ref[...], preferred_element_type=F32)
              + jnp.dot(hb, wl_ref[...], preferred_element_type=F32)
              + jnp.dot(h_lo, wh_ref[...], preferred_element_type=F32)) + br_ref[...]
    lane = lax.broadcasted_iota(jnp.int32, logits.shape, 1)
    logits = jnp.where(lane < n_exp, logits, -jnp.inf)
    m1 = jnp.max(logits, axis=-1, keepdims=True)
    i1 = jnp.min(jnp.where(logits == m1, lane, LANES), axis=-1, keepdims=True)
    rest = jnp.where(lane == i1, -jnp.inf, logits)
    m2 = jnp.max(rest, axis=-1, keepdims=True)
    i2 = jnp.min(jnp.where(rest == m2, lane, LANES), axis=-1, keepdims=True)
    e2 = jnp.exp(m2 - m1)
    g1 = 1.0 / (1.0 + e2)
    g2 = e2 * g1
    rt_ref[0] = jnp.where(lane == 0, g1, jnp.where(lane == 1, g2, jnp.where(
        lane == 2, i1.astype(F32), jnp.where(lane == 3, i2.astype(F32), 0.0))))
    bits = lax.bitcast_convert_type(hf, jnp.uint32)
    nchunk = h.shape[1] // (2 * LANES)
    words = [(bits[:, ci * 2 * LANES:ci * 2 * LANES + LANES] >> 16)
             | bits[:, ci * 2 * LANES + LANES:(ci + 1) * 2 * LANES] for ci in range(nchunk)]
    hp_ref[0] = jnp.concatenate(words, axis=1)


def _router(x, norm_g, shift, scale, w_router, b_router, tm):
    bx, sx, d = x.shape
    n_exp = w_router.shape[1]
    wpad = jnp.zeros((d, LANES), F32).at[:, :n_exp].set(w_router)
    w_hi = wpad.astype(BF16)
    w_lo = (wpad - w_hi.astype(F32)).astype(BF16)
    bpad = jnp.zeros((1, LANES), F32).at[0, :n_exp].set(b_router)
    row = lambda b, i: (b, i, 0)
    vec = lambda b, i: (b, 0, 0)
    const2 = lambda b, i: (0, 0)
    return pl.pallas_call(
        functools.partial(_router_kernel, n_exp=n_exp),
        grid=(bx, sx // tm),
        in_specs=[pl.BlockSpec((1, tm, d), row),
                  pl.BlockSpec((1, d), const2),
                  pl.BlockSpec((1, 1, d), vec),
                  pl.BlockSpec((1, 1, d), vec),
                  pl.BlockSpec((d, LANES), const2),
                  pl.BlockSpec((d, LANES), const2),
                  pl.BlockSpec((1, LANES), const2)],
        out_specs=[pl.BlockSpec((1, tm, d // 2), row), pl.BlockSpec((1, tm, LANES), row)],
        out_shape=[jax.ShapeDtypeStruct((bx, sx, d // 2), jnp.uint32),
                   jax.ShapeDtypeStruct((bx, sx, LANES), F32)],
        compiler_params=_cparams(("parallel", "parallel")),
        name="moe_router",
    )(x, norm_g.reshape(1, d), shift, scale, w_hi, w_lo, bpad)


def _dispatch_kernel(dest_ref, hp_ref, xs_in_ref, xs_ref, sem, *, td):
    del xs_in_ref
    base = pl.program_id(0) * td

    def issue(r, _):
        for kk in range(TOP_K):
            dst = dest_ref[(base + r) * TOP_K + kk]
            pltpu.make_async_copy(hp_ref.at[pl.ds(r, 1)], xs_ref.at[pl.ds(dst, 1)], sem).start()
        return 0

    lax.fori_loop(0, td, issue, 0)

    def drain(r, _):
        for kk in range(TOP_K):
            pltpu.make_async_copy(hp_ref.at[pl.ds(0, 1)], xs_ref.at[pl.ds(0, 1)], sem).wait()
        return 0

    lax.fori_loop(0, td, drain, 0)


def _dispatch(hp, dest, n_rows, td):
    n, w = hp.shape
    xs0 = jnp.zeros((n_rows, w), hp.dtype)
    return pl.pallas_call(
        functools.partial(_dispatch_kernel, td=td),
        grid_spec=pltpu.PrefetchScalarGridSpec(
            num_scalar_prefetch=1,
            grid=(n // td,),
            in_specs=[pl.BlockSpec((td, w), lambda i, dref: (i, 0)),
                      pl.BlockSpec(memory_space=pl.ANY)],
            out_specs=pl.BlockSpec(memory_space=pl.ANY),
            scratch_shapes=[pltpu.SemaphoreType.DMA(())]),
        out_shape=jax.ShapeDtypeStruct((n_rows, w), hp.dtype),
        input_output_aliases={2: 0},
        compiler_params=pltpu.CompilerParams(dimension_semantics=("arbitrary",),
                                             vmem_limit_bytes=VMEM_LIMIT, has_side_effects=True),
        name="moe_dispatch",
    )(dest, hp, xs0)


def _experts_kernel(te_ref, nu_ref, x_ref, w1_ref, w3_ref, w2_ref, o_ref, h_scr, acc_scr):
    i = pl.program_id(0)
    f = pl.program_id(1)

    @pl.when(i < nu_ref[0])
    def _():
        @pl.when(f == 0)
        def _():
            w = x_ref[...]
            nchunk = w.shape[1] // LANES
            for ci in range(nchunk):
                wc = w[:, ci * LANES:(ci + 1) * LANES]
                lo = lax.bitcast_convert_type(wc << 16, F32)
                hi = lax.bitcast_convert_type(wc & jnp.uint32(0xFFFF0000), F32)
                h_scr[:, ci * 2 * LANES:ci * 2 * LANES + LANES] = lo.astype(BF16)
                h_scr[:, ci * 2 * LANES + LANES:(ci + 1) * 2 * LANES] = hi.astype(BF16)
            acc_scr[...] = jnp.zeros_like(acc_scr)

        h = h_scr[...]
        a = jnp.dot(h, w1_ref[0], preferred_element_type=F32)
        b = jnp.dot(h, w3_ref[0], preferred_element_type=F32)
        t = (a * jax.nn.sigmoid(a) * b).astype(BF16)
        acc_scr[...] += jnp.dot(t, w2_ref[0], preferred_element_type=F32)

        @pl.when(f == pl.num_programs(1) - 1)
        def _():
            o_ref[...] = acc_scr[...]

    @pl.when(jnp.logical_and(i >= nu_ref[0], f == 0))
    def _():
        o_ref[...] = jnp.zeros_like(o_ref)


def _experts(xs, tile_expert, n_used, w1_bf, w3_bf, w2_bf, tm, tf):
    n_rows, w = xs.shape
    n_exp, d, ff = w1_bf.shape
    nf = ff // tf
    n_tiles = n_rows // tm

    def live(i, nu):
        return jnp.minimum(i, nu[0] - 1)

    def fsel(i, f, nu):
        return jnp.where(i < nu[0], f, nf - 1)

    return pl.pallas_call(
        _experts_kernel,
        grid_spec=pltpu.PrefetchScalarGridSpec(
            num_scalar_prefetch=2,
            grid=(n_tiles, nf),
            in_specs=[pl.BlockSpec((tm, w), lambda i, f, te, nu: (live(i, nu), 0)),
                      pl.BlockSpec((1, d, tf), lambda i, f, te, nu: (te[live(i, nu)], 0, fsel(i, f, nu))),
                      pl.BlockSpec((1, d, tf), lambda i, f, te, nu: (te[live(i, nu)], 0, fsel(i, f, nu))),
                      pl.BlockSpec((1, tf, d), lambda i, f, te, nu: (te[live(i, nu)], fsel(i, f, nu), 0))],
            out_specs=pl.BlockSpec((tm, d), lambda i, f, te, nu: (i, 0)),
            scratch_shapes=[pltpu.VMEM((tm, d), BF16), pltpu.VMEM((tm, d), F32)]),
        out_shape=jax.ShapeDtypeStruct((n_rows, d), F32),
        compiler_params=_cparams(("arbitrary", "arbitrary")),
        name="moe_experts",
    )(tile_expert, n_used, xs, w1_bf, w3_bf, w2_bf)


def _combine_kernel(pos_ref, x_ref, gate_ref, rt_ref, y_ref, o_ref, ybuf, sem, *, tc):
    base = pl.program_id(0) * tc

    def issue(r, _):
        for kk in range(TOP_K):
            src = pos_ref[(base + r) * TOP_K + kk]
            pltpu.make_async_copy(y_ref.at[pl.ds(src, 1)], ybuf.at[kk, pl.ds(r, 1)], sem).start()
        return 0

    lax.fori_loop(0, tc, issue, 0)

    def drain(r, _):
        for kk in range(TOP_K):
            pltpu.make_async_copy(y_ref.at[pl.ds(0, 1)], ybuf.at[kk, pl.ds(0, 1)], sem).wait()
        return 0

    lax.fori_loop(0, tc, drain, 0)
    rt = rt_ref[...]
    y = rt[:, 0:1] * ybuf[0] + rt[:, 1:2] * ybuf[1]
    o_ref[...] = x_ref[...] + gate_ref[0] * y


def _combine(x2, gate, route, y, pos, seq, tc):
    n, d = x2.shape
    return pl.pallas_call(
        functools.partial(_combine_kernel, tc=tc),
        grid_spec=pltpu.PrefetchScalarGridSpec(
            num_scalar_prefetch=1,
            grid=(n // tc,),
            in_specs=[pl.BlockSpec((tc, d), lambda i, p: (i, 0)),
                      pl.BlockSpec((1, 1, d), lambda i, p: ((i * tc) // seq, 0, 0)),
                      pl.BlockSpec((tc, LANES), lambda i, p: (i, 0)),
                      pl.BlockSpec(memory_space=pl.ANY)],
            out_specs=pl.BlockSpec((tc, d), lambda i, p: (i, 0)),
            scratch_shapes=[pltpu.VMEM((TOP_K, tc, d), F32), pltpu.SemaphoreType.DMA(())]),
        out_shape=jax.ShapeDtypeStruct((n, d), F32),
        compiler_params=_cparams(("arbitrary",)),
        name="moe_combine",
    )(pos, x2, gate, route, y)


def _route_positions(e_flat, n_exp, tm):
    a = e_flat.shape[0]
    onehot = (e_flat[:, None] == jnp.arange(n_exp, dtype=jnp.int32)[None, :]).astype(jnp.int32)
    csum = jnp.cumsum(onehot, axis=0)
    counts = csum[-1]
    padded = ((counts + tm - 1) // tm) * tm
    ends = jnp.cumsum(padded)
    starts = ends - padded
    dest = jnp.sum(onehot * (csum - 1 + starts[None, :]), axis=1).astype(jnp.int32)
    n_tiles = a // tm + n_exp
    tile_ids = jnp.arange(n_tiles, dtype=jnp.int32)
    tile_expert = jnp.minimum(jnp.sum((ends[None, :] // tm <= tile_ids[:, None]).astype(jnp.int32), axis=1),
                              n_exp - 1)
    n_used = (ends[-1] // tm).astype(jnp.int32).reshape(1)
    return dest, tile_expert.astype(jnp.int32), n_used, n_tiles * tm


def _blockdiag_groups(w, ngroups):
    nblk, bw, _ = w.shape
    per = nblk // ngroups
    eye = jnp.eye(per, dtype=w.dtype)
    wg = w.reshape(ngroups, per, bw, bw)
    full = wg[:, :, :, None, :] * eye[None, :, None, :, None]
    return full.reshape(ngroups, per * bw, per * bw)


def kernel(x, c, ctx, c_ctx, l0_w_mod, l0_b_mod, l0_norm1, l0_norm2, l0_w_qkv, l0_q_gain, l0_k_gain, l0_rpb, l0_w_o, l0_ffn_w1, l0_ffn_w3, l0_ffn_w2, l1_w_mod, l1_b_mod, l1_norm1, l1_norm2, l1_w_in, l1_conv_w, l1_conv_b, l1_gate_a_w, l1_gate_a_b, l1_gate_x_w, l1_gate_x_b, l1_lam, l1_w_out, l1_router_w, l1_router_b, l1_moe_w1, l1_moe_w3, l1_moe_w2):
    b, s, d = x.shape
    n_ctx = ctx.shape[1]
    head_dim = l0_q_gain.shape[0]
    n_heads = d // head_dim
    n_exp = l1_router_w.shape[1]
    tm = min(512, s)
    tmc = min(512, n_ctx)

    def mods(w_mod, b_mod):
        m = _adaln_mod(jnp.concatenate([c, c_ctx[None, :]], axis=0), w_mod, b_mod)
        lat = [m[:b, j * d:(j + 1) * d].reshape(b, 1, d) for j in range(6)]
        cx = [jnp.broadcast_to(m[b:b + 1, j * d:(j + 1) * d].reshape(1, 1, d), (b, 1, d)) for j in range(6)]
        return lat, cx

    ml, mc = mods(l0_w_mod, l0_b_mod)
    w_qkv = l0_w_qkv.astype(BF16)
    qg = (jnp.tile(l0_q_gain, n_heads) * (head_dim ** -0.5)).reshape(1, d)
    kg = jnp.tile(l0_k_gain, n_heads).reshape(1, d)
    ql, kl, vl = _qkv(x, l0_norm1, ml[0], ml[1], w_qkv, qg, kg, head_dim, tm)
    qc, kc, vc = _qkv(ctx, l0_norm1, mc[0], mc[1], w_qkv, qg, kg, head_dim, tmc)
    o_l = _attention(ql, kl, vl, kc, vc, _att_bias_table(l0_rpb))
    o_c = _ctx_attention(qc, kc, vc)
    w_o = l0_w_o.astype(BF16)
    x_l = _proj_residual(o_l, w_o, x, ml[2], tm)
    x_c = _proj_residual(o_c, w_o, ctx, mc[2], tmc)
    w1, w3, w2 = l0_ffn_w1.astype(BF16), l0_ffn_w3.astype(BF16), l0_ffn_w2.astype(BF16)
    tf0 = l0_ffn_w1.shape[1] // 2
    x_l = _ffn(x_l, l0_norm2, ml[3], ml[4], ml[5], w1, w3, w2, tm, tf0)
    x_c = _ffn(x_c, l0_norm2, mc[3], mc[4], mc[5], w1, w3, w2, tmc, tf0)

    ml, mc = mods(l1_w_mod, l1_b_mod)
    w_in = l1_w_in.astype(BF16)
    g_l, xr_l = _inproj(x_l, l1_norm1, ml[0], ml[1], w_in, tm)
    _, xr_c = _inproj(x_c, l1_norm1, mc[0], mc[1], w_in, tmc)
    dr = xr_l.shape[-1]
    ngroups = 4
    t_scan = min(256, n_ctx)
    zeros0 = jnp.zeros((b, 1, dr), F32)
    hs = []
    for di, rev in enumerate((False, True)):
        wa_g = _blockdiag_groups(l1_gate_a_w[di], ngroups).astype(BF16)
        wx_g = _blockdiag_groups(l1_gate_x_w[di], ngroups).astype(BF16)
        args = (l1_conv_w, l1_conv_b, wa_g, wx_g, l1_gate_a_b[di], l1_gate_x_b[di], l1_lam[di])
        h_c = _lru_scan(xr_c, *args, zeros0, rev, t_scan)
        h0 = h_c[:, 0:1, :] if rev else h_c[:, n_ctx - 1:n_ctx, :]
        hs.append(_lru_scan(xr_l, *args, h0, rev, t_scan))
    x_l = _lru_out(g_l, hs[0], hs[1], l1_w_out.astype(BF16), x_l, ml[2], tm)

    hp, route = _router(x_l, l1_norm2, ml[3], ml[4], l1_router_w, l1_router_b, tm)
    n = b * s
    route2 = route.reshape(n, LANES)
    e_flat = route2[:, 2:2 + TOP_K].astype(jnp.int32).reshape(n * TOP_K)
    tme = 512
    dest, tile_expert, n_used, n_rows = _route_positions(e_flat, n_exp, tme)
    xs = _dispatch(hp.reshape(n, d // 2), dest, n_rows, min(512, n))
    y = _experts(xs, tile_expert, n_used, l1_moe_w1.astype(BF16), l1_moe_w3.astype(BF16),
                 l1_moe_w2.astype(BF16), tme, 512)
    out = _combine(x_l.reshape(n, d), ml[5], route2, y, dest, s, min(256, n))
    return out.reshape(b, s, d)
```

```python
import functools

import numpy as np
import jax
import jax.numpy as jnp
from jax import lax
from jax.experimental import pallas as pl
from jax.experimental.pallas import tpu as pltpu

GRID_W = 64
WIN_H = 8
WIN_W = 16
TOP_K = 2
LRU_C = 8.0
EPS = 1e-6
CONV_LEFT = 2
NEG_BIG = -1e30
LANES = 128
SUBLANES = 8
VMEM_LIMIT = 56 * 1024 * 1024

ATT_ROWS = 8
ATT_KROWS = 16
ATT_PAIR = 2
ATT_BAND = 10

F32 = jnp.float32
BF16 = jnp.bfloat16


def _cparams(sem):
    return pltpu.CompilerParams(dimension_semantics=sem, vmem_limit_bytes=VMEM_LIMIT)


def _rms_mod(x, g, shift, scale):
    ms = jnp.mean(x * x, axis=-1, keepdims=True)
    y = x * lax.rsqrt(ms + EPS) * g
    return y * (1.0 + scale) + shift


def _mod_kernel(ct_ref, w_ref, b_ref, o_ref, *, nrows):
    ct = ct_ref[...]
    s = ct * jax.nn.sigmoid(ct)
    w = w_ref[...]
    rows = [jnp.sum(w * s[:, m:m + 1], axis=0, keepdims=True) + b_ref[...] for m in range(nrows)]
    rows += [jnp.zeros_like(rows[0])] * (SUBLANES - nrows)
    o_ref[...] = jnp.concatenate(rows, axis=0)


def _adaln_mod(cvecs, w_mod, b_mod):
    nrows, d = cvecs.shape
    n = w_mod.shape[1]
    tn = 768
    ct = jnp.zeros((d, SUBLANES), F32).at[:, :nrows].set(cvecs.T)
    return pl.pallas_call(
        functools.partial(_mod_kernel, nrows=nrows),
        grid=(n // tn,),
        in_specs=[pl.BlockSpec((d, SUBLANES), lambda j: (0, 0)),
                  pl.BlockSpec((d, tn), lambda j: (0, j)),
                  pl.BlockSpec((1, tn), lambda j: (0, j))],
        out_specs=pl.BlockSpec((SUBLANES, tn), lambda j: (0, j)),
        out_shape=jax.ShapeDtypeStruct((SUBLANES, n), F32),
        compiler_params=_cparams(("arbitrary",)),
        name="adaln_mod",
    )(ct, w_mod, b_mod.reshape(1, n))


def _qkv_kernel(x_ref, g_ref, sh_ref, sc_ref, w_ref, qg_ref, kg_ref, pm_ref, pe_ref,
                q_ref, k_ref, v_ref, *, d):
    h = _rms_mod(x_ref[0], g_ref[...], sh_ref[0], sc_ref[0]).astype(BF16)
    for part, (gain_ref, out_ref) in enumerate(((qg_ref, q_ref), (kg_ref, k_ref))):
        y = jnp.dot(h, w_ref[:, part * d:(part + 1) * d], preferred_element_type=F32)
        ms = jnp.dot((y * y).astype(BF16), pm_ref[...], preferred_element_type=F32)
        inv = lax.rsqrt(ms + EPS)
        inv_hi = inv.astype(BF16)
        inv_lo = (inv - inv_hi.astype(F32)).astype(BF16)
        full = (jnp.dot(inv_hi, pe_ref[...], preferred_element_type=F32)
                + jnp.dot(inv_lo, pe_ref[...], preferred_element_type=F32))
        out_ref[0] = (y * full * gain_ref[...]).astype(BF16)
    v_ref[0] = jnp.dot(h, w_ref[:, 2 * d:3 * d], preferred_element_type=F32).astype(BF16)


def _qkv(x, norm_g, shift, scale, w_qkv_bf, q_gain_full, k_gain_full, head_dim, tm):
    bx, sx, d = x.shape
    n_heads = d // head_dim
    head_of = np.arange(d) // head_dim
    pm = np.zeros((d, LANES), np.float32)
    pm[np.arange(d), head_of] = 1.0 / head_dim
    pe = np.zeros((LANES, d), np.float32)
    pe[head_of, np.arange(d)] = 1.0
    assert n_heads <= LANES
    row = lambda b, i: (b, i, 0)
    vec = lambda b, i: (b, 0, 0)
    const2 = lambda b, i: (0, 0)
    out = jax.ShapeDtypeStruct((bx, sx, d), BF16)
    return pl.pallas_call(
        functools.partial(_qkv_kernel, d=d),
        grid=(bx, sx // tm),
        in_specs=[pl.BlockSpec((1, tm, d), row),
                  pl.BlockSpec((1, d), const2),
                  pl.BlockSpec((1, 1, d), vec),
                  pl.BlockSpec((1, 1, d), vec),
                  pl.BlockSpec((d, 3 * d), const2),
                  pl.BlockSpec((1, d), const2),
                  pl.BlockSpec((1, d), const2),
                  pl.BlockSpec((d, LANES), const2),
                  pl.BlockSpec((LANES, d), const2)],
        out_specs=[pl.BlockSpec((1, tm, d), row)] * 3,
        out_shape=[out, out, out],
        compiler_params=_cparams(("parallel", "parallel")),
        name="qkv_headnorm",
    )(x, norm_g.reshape(1, d), shift, scale, w_qkv_bf, q_gain_full, k_gain_full,
      jnp.asarray(pm, BF16), jnp.asarray(pe, BF16))


def _att_band_layout(rows):
    npair = ATT_ROWS // ATT_PAIR
    half_win = WIN_H // 2
    sets = []
    for kind, off in (("mid", half_win), ("top", 0), ("bot", ATT_KROWS - ATT_ROWS)):
        for p in range(npair):
            if kind == "mid" and p > 0:
                continue
            rs_rel = []
            for a in range(ATT_PAIR):
                rq = ATT_PAIR * p + a
                if kind == "mid":
                    rs_rel.append(rq)
                elif kind == "top":
                    rs_rel.append(max(rq - half_win, 0))
                else:
                    rs_rel.append(min(rq + off - half_win, ATT_KROWS - WIN_H))
            m0 = min(rs_rel[0] // 2, (ATT_KROWS - ATT_BAND) // 2)
            sets.append((kind, p, off, rs_rel, m0))
    return sets


def _att_bias_table(rpb):
    n_heads, n_dr, n_dc = rpb.shape
    sets = _att_band_layout(None)
    cq = np.arange(GRID_W)
    cs = np.clip(cq - WIN_W // 2, 0, GRID_W - WIN_W)
    ck = np.arange(GRID_W)
    col_ok = (ck[None, :] >= cs[:, None]) & (ck[None, :] < cs[:, None] + WIN_W)
    dc = ck[None, :] - cq[:, None] + (WIN_W - 1)
    sel = ((dc[:, None, :] == np.arange(n_dc)[None, :, None]) & col_ok[:, None, :]).astype(np.float32)
    blocks = jnp.einsum("hrx,qxk->hrqk", rpb.astype(F32), jnp.asarray(sel), precision=lax.Precision.HIGHEST)
    blocks = jnp.where(jnp.asarray(col_ok)[None, None], blocks, NEG_BIG)
    blocks = jnp.concatenate([blocks, jnp.full((n_heads, 1, GRID_W, GRID_W), NEG_BIG, F32)], axis=1)
    which = np.full((len(sets), ATT_PAIR, ATT_BAND), n_dr, np.int32)
    for si, (kind, p, off, rs_rel, m0) in enumerate(sets):
        for a in range(ATT_PAIR):
            rq = ATT_PAIR * p + a if kind != "mid" else a
            rs = rs_rel[a] if kind != "mid" else a
            base = m0 if kind != "mid" else 0
            for j in range(ATT_BAND):
                rk = 2 * base + j
                if rs <= rk < rs + WIN_H:
                    which[si, a, j] = rk - rq - off + (WIN_H - 1)
    tab = jnp.take(blocks, jnp.asarray(which.reshape(-1)), axis=1)
    tab = tab.reshape(n_heads, len(sets), ATT_PAIR, ATT_BAND, GRID_W, GRID_W)
    tab = tab.reshape(n_heads // 2, 2, len(sets), ATT_PAIR, ATT_BAND, GRID_W, GRID_W)
    tab = tab.transpose(0, 2, 1, 3, 5, 4, 6)
    return tab.reshape(n_heads // 2, len(sets), 2 * ATT_PAIR * GRID_W, ATT_BAND * GRID_W)


def _attn_kernel(q_ref, k0, k1, k2, k3, v0, v1, v2, v3, kc_ref, vc_ref, tb_ref, o_ref,
                 kwin, vwin, *, m0_top, m0_bot):
    i = pl.program_id(2)
    nt = pl.num_programs(2)
    ck = k0.shape[2]
    for c, (kr, vr) in enumerate(((k0, v0), (k1, v1), (k2, v2), (k3, v3))):
        kwin[c * ck:(c + 1) * ck, :] = kr[0, 0]
        vwin[c * ck:(c + 1) * ck, :] = vr[0, 0]
    kc = kc_ref[0]
    vc = vc_ref[0]
    lane = lax.broadcasted_iota(jnp.int32, (1, LANES), 1)
    head_lo = lane < (LANES // 2)
    nq = ATT_PAIR * GRID_W
    nk = ATT_BAND * GRID_W
    npair = ATT_ROWS // ATT_PAIR
    dn = (((1,), (1,)), ((), ()))
    for p in range(npair):
        m0 = jnp.where(i == 0, m0_top[p], jnp.where(i == nt - 1, m0_bot[p], p))
        st = jnp.where(i == 0, 1 + p, jnp.where(i == nt - 1, 1 + npair + p, 0))
        start = pl.multiple_of(m0 * LANES, LANES)
        kb = kwin[pl.ds(start, nk), :]
        vb = vwin[pl.ds(start, nk), :]
        qp = q_ref[0, p * nq:(p + 1) * nq, :]
        zero = jnp.zeros_like(qp)
        q2 = jnp.concatenate([jnp.where(head_lo, qp, zero), jnp.where(head_lo, zero, qp)], axis=0)
        s = lax.dot_general(q2, kb, dn, preferred_element_type=F32) + tb_ref[0, st]
        sc = lax.dot_general(q2, kc, dn, preferred_element_type=F32)
        m = jnp.maximum(jnp.max(s, axis=-1, keepdims=True), jnp.max(sc, axis=-1, keepdims=True))
        e = jnp.exp(s - m)
        ec = jnp.exp(sc - m)
        den = jnp.sum(e, axis=-1, keepdims=True) + jnp.sum(ec, axis=-1, keepdims=True)
        o2 = (jnp.dot(e.astype(BF16), vb, preferred_element_type=F32)
              + jnp.dot(ec.astype(BF16), vc, preferred_element_type=F32)) * (1.0 / den)
        o_ref[0, p * nq:(p + 1) * nq, :] = jnp.where(head_lo, o2[:nq], o2[nq:]).astype(BF16)


def _attention(q, k, v, kc, vc, bias_tab):
    b, s, d = q.shape
    c = kc.shape[1]
    rows = s // GRID_W
    assert rows % ATT_ROWS == 0 and rows >= ATT_KROWS
    tq = ATT_ROWS * GRID_W
    nt = s // tq
    chunk_rows = 4
    ck = chunk_rows * GRID_W
    nchunk = ATT_KROWS // chunk_rows
    k4 = k.reshape(b, s // ck, ck, d)
    v4 = v.reshape(b, s // ck, ck, d)
    sets = _att_band_layout(None)
    npair = ATT_ROWS // ATT_PAIR
    m0_top = tuple(sets[1 + p][4] for p in range(npair))
    m0_bot = tuple(sets[1 + npair + p][4] for p in range(npair))

    def kmap(t):
        def f(bi, j, i):
            c0 = jnp.clip(i * (ATT_ROWS // chunk_rows) - (WIN_H // 2) // chunk_rows, 0,
                          (rows - ATT_KROWS) // chunk_rows)
            return (bi, c0 + t, 0, j)
        return f

    kv_specs = [pl.BlockSpec((1, 1, ck, LANES), kmap(t)) for t in range(nchunk)]
    n_sets = bias_tab.shape[1]
    return pl.pallas_call(
        functools.partial(_attn_kernel, m0_top=m0_top, m0_bot=m0_bot),
        grid=(b, d // LANES, nt),
        in_specs=[pl.BlockSpec((1, tq, LANES), lambda bi, j, i: (bi, i, j))]
                 + kv_specs + kv_specs
                 + [pl.BlockSpec((1, c, LANES), lambda bi, j, i: (bi, 0, j)),
                    pl.BlockSpec((1, c, LANES), lambda bi, j, i: (bi, 0, j)),
                    pl.BlockSpec((1, n_sets, 2 * ATT_PAIR * GRID_W, ATT_BAND * GRID_W),
                                 lambda bi, j, i: (j, 0, 0, 0))],
        out_specs=pl.BlockSpec((1, tq, LANES), lambda bi, j, i: (bi, i, j)),
        out_shape=jax.ShapeDtypeStruct((b, s, d), BF16),
        scratch_shapes=[pltpu.VMEM((ATT_KROWS * GRID_W, LANES), BF16),
                        pltpu.VMEM((ATT_KROWS * GRID_W, LANES), BF16)],
        compiler_params=_cparams(("parallel", "parallel", "arbitrary")),
        name="nbr_attention",
    )(q, *([k4] * nchunk), *([v4] * nchunk), kc, vc, bias_tab)


def _ctx_attn_kernel(q_ref, k_ref, v_ref, o_ref):
    lane = lax.broadcasted_iota(jnp.int32, (1, LANES), 1)
    head_lo = lane < (LANES // 2)
    q = q_ref[0]
    k = k_ref[0]
    v = v_ref[0]
    outs = []
    for hh in range(2):
        mask = head_lo if hh == 0 else jnp.logical_not(head_lo)
        qh = jnp.where(mask, q, jnp.zeros_like(q))
        s = lax.dot_general(qh, k, (((1,), (1,)), ((), ())), preferred_element_type=F32)
        m = jnp.max(s, axis=-1, keepdims=True)
        e = jnp.exp(s - m)
        den = jnp.sum(e, axis=-1, keepdims=True)
        o = jnp.dot(e.astype(BF16), v, preferred_element_type=F32)
        outs.append(o * (1.0 / den))
    o_ref[0] = jnp.where(head_lo, outs[0], outs[1]).astype(BF16)


def _ctx_attention(q, k, v):
    b, c, d = q.shape
    spec = pl.BlockSpec((1, c, LANES), lambda bi, j: (bi, 0, j))
    return pl.pallas_call(
        _ctx_attn_kernel,
        grid=(b, d // LANES),
        in_specs=[spec, spec, spec],
        out_specs=spec,
        out_shape=jax.ShapeDtypeStruct((b, c, d), BF16),
        compiler_params=_cparams(("parallel", "parallel")),
        name="ctx_attention",
    )(q, k, v)


def _proj_res_kernel(a_ref, w_ref, x_ref, gate_ref, o_ref):
    y = jnp.dot(a_ref[0], w_ref[...], preferred_element_type=F32)
    o_ref[0] = x_ref[0] + gate_ref[0] * y


def _proj_residual(a, w_bf, x, gate, tm):
    bx, sx, kdim = a.shape
    d = x.shape[-1]
    row = lambda b, i: (b, i, 0)
    return pl.pallas_call(
        _proj_res_kernel,
        grid=(bx, sx // tm),
        in_specs=[pl.BlockSpec((1, tm, kdim), row),
                  pl.BlockSpec((kdim, d), lambda b, i: (0, 0)),
                  pl.BlockSpec((1, tm, d), row),
                  pl.BlockSpec((1, 1, d), lambda b, i: (b, 0, 0))],
        out_specs=pl.BlockSpec((1, tm, d), row),
        out_shape=jax.ShapeDtypeStruct(x.shape, F32),
        compiler_params=_cparams(("parallel", "parallel")),
        name="proj_residual",
    )(a, w_bf, x, gate)


def _swiglu_chunks(h, w1_at, w3_at, w2_at, ff, tf):
    acc = None
    for c0 in range(0, ff, tf):
        a = jnp.dot(h, w1_at(c0, tf), preferred_element_type=F32)
        b = jnp.dot(h, w3_at(c0, tf), preferred_element_type=F32)
        t = (a * jax.nn.sigmoid(a) * b).astype(BF16)
        y = jnp.dot(t, w2_at(c0, tf), preferred_element_type=F32)
        acc = y if acc is None else acc + y
    return acc


def _ffn_kernel(x_ref, g_ref, sh_ref, sc_ref, gate_ref, w1_ref, w3_ref, w2_ref, o_ref, *, tf):
    x = x_ref[0]
    h = _rms_mod(x, g_ref[...], sh_ref[0], sc_ref[0]).astype(BF16)
    y = _swiglu_chunks(h, lambda c0, n: w1_ref[:, c0:c0 + n], lambda c0, n: w3_ref[:, c0:c0 + n],
                       lambda c0, n: w2_ref[c0:c0 + n, :], w1_ref.shape[1], tf)
    o_ref[0] = x + gate_ref[0] * y


def _ffn(x, norm_g, shift, scale, gate, w1_bf, w3_bf, w2_bf, tm, tf):
    bx, sx, d = x.shape
    ff = w1_bf.shape[1]
    row = lambda b, i: (b, i, 0)
    vec = lambda b, i: (b, 0, 0)
    const2 = lambda b, i: (0, 0)
    once = pl.Buffered(1)
    return pl.pallas_call(
        functools.partial(_ffn_kernel, tf=tf),
        grid=(bx, sx // tm),
        in_specs=[pl.BlockSpec((1, tm, d), row),
                  pl.BlockSpec((1, d), const2),
                  pl.BlockSpec((1, 1, d), vec),
                  pl.BlockSpec((1, 1, d), vec),
                  pl.BlockSpec((1, 1, d), vec),
                  pl.BlockSpec((d, ff), const2, pipeline_mode=once),
                  pl.BlockSpec((d, ff), const2, pipeline_mode=once),
                  pl.BlockSpec((ff, d), const2, pipeline_mode=once)],
        out_specs=pl.BlockSpec((1, tm, d), row),
        out_shape=jax.ShapeDtypeStruct(x.shape, F32),
        compiler_params=_cparams(("parallel", "parallel")),
        name="ffn_swiglu",
    )(x, norm_g.reshape(1, d), shift, scale, gate, w1_bf, w3_bf, w2_bf)


def _inproj_kernel(x_ref, g_ref, sh_ref, sc_ref, w_ref, gb_ref, xr_ref, *, dr):
    h = _rms_mod(x_ref[0], g_ref[...], sh_ref[0], sc_ref[0]).astype(BF16)
    gb_ref[0] = jnp.dot(h, w_ref[:, :dr], preferred_element_type=F32)
    xr_ref[0] = jnp.dot(h, w_ref[:, dr:], preferred_element_type=F32)


def _inproj(x, norm_g, shift, scale, w_in_bf, tm):
    bx, sx, d = x.shape
    dr = w_in_bf.shape[1] // 2
    row = lambda b, i: (b, i, 0)
    vec = lambda b, i: (b, 0, 0)
    out = jax.ShapeDtypeStruct((bx, sx, dr), F32)
    return pl.pallas_call(
        functools.partial(_inproj_kernel, dr=dr),
        grid=(bx, sx // tm),
        in_specs=[pl.BlockSpec((1, tm, d), row),
                  pl.BlockSpec((1, d), lambda b, i: (0, 0)),
                  pl.BlockSpec((1, 1, d), vec),
                  pl.BlockSpec((1, 1, d), vec),
                  pl.BlockSpec((d, 2 * dr), lambda b, i: (0, 0))],
        out_specs=[pl.BlockSpec((1, tm, dr), row)] * 2,
        out_shape=[out, out],
        compiler_params=_cparams(("parallel", "parallel")),
        name="rglru_inproj",
    )(x, norm_g.reshape(1, d), shift, scale, w_in_bf)


def _lru_scan_kernel(x_ref, xp_ref, xn_ref, cw_ref, cb_ref, wa_ref, wx_ref, ba_ref, bx_ref, lam_ref,
                     h0_ref, o_ref, ext, a_scr, u_scr, carry, *, reverse, ngroups):
    i = pl.program_id(1)
    nb = pl.num_programs(1)
    blk = (nb - 1 - i) if reverse else i
    t, c = x_ref.shape[1], x_ref.shape[2]
    pad = SUBLANES

    @pl.when(i == 0)
    def _():
        carry[...] = jnp.broadcast_to(h0_ref[0], carry.shape)

    ext[0:pad, :] = jnp.where(blk > 0, xp_ref[0], 0.0)
    ext[pad:pad + t, :] = x_ref[0]
    ext[pad + t:pad + t + pad, :] = jnp.where(blk < nb - 1, xn_ref[0], 0.0)
    cw = cw_ref[...]
    xc = cb_ref[...] + sum(cw[j:j + 1, :] * ext[pad + j - CONV_LEFT:pad + j - CONV_LEFT + t, :]
                           for j in range(cw.shape[0]))

    xb = xc.astype(BF16)
    gw = c // ngroups
    pre_a, pre_x = [], []
    for gi in range(ngroups):
        xg = xb[:, gi * gw:(gi + 1) * gw]
        pre_a.append(jnp.dot(xg, wa_ref[gi], preferred_element_type=F32))
        pre_x.append(jnp.dot(xg, wx_ref[gi], preferred_element_type=F32))
    r = jax.nn.sigmoid(jnp.concatenate(pre_a, axis=1) + ba_ref[...])
    gi_ = jax.nn.sigmoid(jnp.concatenate(pre_x, axis=1) + bx_ref[...])
    lam = lam_ref[...]
    sp = jnp.maximum(-lam, 0.0) + jnp.log(1.0 + jnp.exp(-jnp.abs(lam)))
    a = jnp.exp((-LRU_C) * r * sp)
    u = jnp.sqrt(1.0 - a * a) * (gi_ * xc)

    sub = lax.broadcasted_iota(jnp.int32, (t, 1), 0) % SUBLANES
    for sft in (1, 2, 4):
        if reverse:
            a_sh = pltpu.roll(a, t - sft, axis=0)
            u_sh = pltpu.roll(u, t - sft, axis=0)
            ok = sub < SUBLANES - sft
        else:
            a_sh = pltpu.roll(a, sft, axis=0)
            u_sh = pltpu.roll(u, sft, axis=0)
            ok = sub >= sft
        u = jnp.where(ok, a * u_sh + u, u)
        a = jnp.where(ok, a * a_sh, a)
    a_scr[...] = a
    u_scr[...] = u
    ng = t // SUBLANES
    edge = 0 if reverse else SUBLANES - 1

    def body(j, hprev):
        g = (ng - 1 - j) if reverse else j
        off = pl.multiple_of(g * SUBLANES, SUBLANES)
        hblk = a_scr[pl.ds(off, SUBLANES), :] * hprev + u_scr[pl.ds(off, SUBLANES), :]
        o_ref[0, pl.ds(off, SUBLANES), :] = hblk
        return jnp.broadcast_to(hblk[edge:edge + 1, :], hblk.shape)

    carry[...] = lax.fori_loop(0, ng, body, carry[...])


def _lru_scan(xr, conv_w, conv_b, wa_g, wx_g, ba, bx, lam, h0, reverse, t):
    b, s, c = xr.shape
    nb = s // t
    ngroups = wa_g.shape[0]
    gw = c // ngroups
    tb = t // SUBLANES
    if reverse:
        pos = lambda i: nb - 1 - i
    else:
        pos = lambda i: i
    const2 = lambda bi, i: (0, 0)
    const3 = lambda bi, i: (0, 0, 0)
    return pl.pallas_call(
        functools.partial(_lru_scan_kernel, reverse=reverse, ngroups=ngroups),
        grid=(b, nb),
        in_specs=[pl.BlockSpec((1, t, c), lambda bi, i: (bi, pos(i), 0)),
                  pl.BlockSpec((1, SUBLANES, c), lambda bi, i: (bi, jnp.maximum(pos(i) * tb - 1, 0), 0)),
                  pl.BlockSpec((1, SUBLANES, c),
                               lambda bi, i: (bi, jnp.minimum((pos(i) + 1) * tb, s // SUBLANES - 1), 0)),
                  pl.BlockSpec(conv_w.shape, const2),
                  pl.BlockSpec((1, c), const2),
                  pl.BlockSpec((ngroups, gw, gw), const3),
                  pl.BlockSpec((ngroups, gw, gw), const3),
                  pl.BlockSpec((1, c), const2),
                  pl.BlockSpec((1, c), const2),
                  pl.BlockSpec((1, c), const2),
                  pl.BlockSpec((1, 1, c), lambda bi, i: (bi, 0, 0))],
        out_specs=pl.BlockSpec((1, t, c), lambda bi, i: (bi, pos(i), 0)),
        out_shape=jax.ShapeDtypeStruct((b, s, c), F32),
        scratch_shapes=[pltpu.VMEM((t + 2 * SUBLANES, c), F32),
                        pltpu.VMEM((t, c), F32),
                        pltpu.VMEM((t, c), F32),
                        pltpu.VMEM((SUBLANES, c), F32)],
        compiler_params=_cparams(("parallel", "arbitrary")),
        name="lru_scan_bwd" if reverse else "lru_scan_fwd",
    )(xr, xr, xr, conv_w, conv_b.reshape(1, c), wa_g, wx_g, ba.reshape(1, c), bx.reshape(1, c),
      lam.reshape(1, c), h0)


def _lru_out_kernel(g_ref, hf_ref, hb_ref, w_ref, x_ref, gate_ref, o_ref):
    a = (jax.nn.gelu(g_ref[0]) * (hf_ref[0] + hb_ref[0])).astype(BF16)
    y = jnp.dot(a, w_ref[...], preferred_element_type=F32)
    o_ref[0] = x_ref[0] + gate_ref[0] * y


def _lru_out(g, hf, hb, w_out_bf, x, gate, tm):
    bx, sx, dr = g.shape
    d = x.shape[-1]
    row = lambda b, i: (b, i, 0)
    return pl.pallas_call(
        _lru_out_kernel,
        grid=(bx, sx // tm),
        in_specs=[pl.BlockSpec((1, tm, dr), row)] * 3
                 + [pl.BlockSpec((dr, d), lambda b, i: (0, 0)),
                    pl.BlockSpec((1, tm, d), row),
                    pl.BlockSpec((1, 1, d), lambda b, i: (b, 0, 0))],
        out_specs=pl.BlockSpec((1, tm, d), row),
        out_shape=jax.ShapeDtypeStruct(x.shape, F32),
        compiler_params=_cparams(("parallel", "parallel")),
        name="rglru_outproj",
    )(g, hf, hb, w_out_bf, x, gate)


def _router_kernel(x_ref, g_ref, sh_ref, sc_ref, wh_ref, wl_ref, br_ref, hp_ref, rt_ref, *, n_exp):
    h = _rms_mod(x_ref[0], g_ref[...], sh_ref[0], sc_ref[0])
    hb = h.astype(BF16)
    hf = hb.astype(F32)
    h_lo = (h - hf).astype(BF16)
    logits = (jnp.dot(hb, wh_ref[...], preferred_element_type=F32)
              + jnp.dot(hb, wl_ref[...], preferred_element_type=F32)
              + jnp.dot(h_lo, wh_ref[...], preferred_element_type=F32)) + br_ref[...]
    lane = lax.broadcasted_iota(jnp.int32, logits.shape, 1)
    logits = jnp.where(lane < n_exp, logits, -jnp.inf)
    m1 = jnp.max(logits, axis=-1, keepdims=True)
    i1 = jnp.min(jnp.where(logits == m1, lane, LANES), axis=-1, keepdims=True)
    rest = jnp.where(lane == i1, -jnp.inf, logits)
    m2 = jnp.max(rest, axis=-1, keepdims=True)
    i2 = jnp.min(jnp.where(rest == m2, lane, LANES), axis=-1, keepdims=True)
    e2 = jnp.exp(m2 - m1)
    g1 = 1.0 / (1.0 + e2)
    g2 = e2 * g1
    rt_ref[0] = jnp.where(lane == 0, g1, jnp.where(lane == 1, g2, jnp.where(
        lane == 2, i1.astype(F32), jnp.where(lane == 3, i2.astype(F32), 0.0))))
    hp_ref[0] = h


def _router(x, norm_g, shift, scale, w_router, b_router, tm):
    bx, sx, d = x.shape
    n_exp = w_router.shape[1]
    wpad = jnp.zeros((d, LANES), F32).at[:, :n_exp].set(w_router)
    w_hi = wpad.astype(BF16)
    w_lo = (wpad - w_hi.astype(F32)).astype(BF16)
    bpad = jnp.zeros((1, LANES), F32).at[0, :n_exp].set(b_router)
    row = lambda b, i: (b, i, 0)
    vec = lambda b, i: (b, 0, 0)
    const2 = lambda b, i: (0, 0)
    return pl.pallas_call(
        functools.partial(_router_kernel, n_exp=n_exp),
        grid=(bx, sx // tm),
        in_specs=[pl.BlockSpec((1, tm, d), row),
                  pl.BlockSpec((1, d), const2),
                  pl.BlockSpec((1, 1, d), vec),
                  pl.BlockSpec((1, 1, d), vec),
                  pl.BlockSpec((d, LANES), const2),
                  pl.BlockSpec((d, LANES), const2),
                  pl.BlockSpec((1, LANES), const2)],
        out_specs=[pl.BlockSpec((1, tm, d), row), pl.BlockSpec((1, tm, LANES), row)],
        out_shape=[jax.ShapeDtypeStruct((bx, sx, d), F32),
                   jax.ShapeDtypeStruct((bx, sx, LANES), F32)],
        compiler_params=_cparams(("parallel", "parallel")),
        name="moe_router",
    )(x, norm_g.reshape(1, d), shift, scale, w_hi, w_lo, bpad)


def _dispatch_kernel(dest_ref, fill_ref, hp_ref, xs_ref, zrow, sem, zsem, *, td, n_fill):
    base = pl.program_id(0) * td

    @pl.when(pl.program_id(0) == 0)
    def _():
        zrow[...] = jnp.zeros_like(zrow)

        def zissue(r, _):
            pltpu.make_async_copy(zrow.at[pl.ds(0, 1)], xs_ref.at[pl.ds(fill_ref[r], 1)], zsem).start()
            return 0

        lax.fori_loop(0, n_fill, zissue, 0, unroll=8)

    def issue(r, _):
        for kk in range(TOP_K):
            dst = dest_ref[(base + r) * TOP_K + kk]
            pltpu.make_async_copy(hp_ref.at[pl.ds(r, 1)], xs_ref.at[pl.ds(dst, 1)], sem).start()
        return 0

    lax.fori_loop(0, td, issue, 0, unroll=8)
    for kk in range(TOP_K):
        pltpu.make_async_copy(hp_ref, xs_ref.at[pl.ds(0, td)], sem).wait()

    @pl.when(pl.program_id(0) == 0)
    def _():
        pltpu.make_async_copy(xs_ref.at[pl.ds(0, n_fill)], xs_ref.at[pl.ds(0, n_fill)], zsem).wait()


def _dispatch(hp, dest, fill, n_rows, td):
    n, w = hp.shape
    n_fill = fill.shape[0]
    return pl.pallas_call(
        functools.partial(_dispatch_kernel, td=td, n_fill=n_fill),
        grid_spec=pltpu.PrefetchScalarGridSpec(
            num_scalar_prefetch=2,
            grid=(n // td,),
            in_specs=[pl.BlockSpec((td, w), lambda i, dref, fref: (i, 0))],
            out_specs=pl.BlockSpec(memory_space=pl.ANY),
            scratch_shapes=[pltpu.VMEM((SUBLANES, w), hp.dtype),
                            pltpu.SemaphoreType.DMA(()), pltpu.SemaphoreType.DMA(())]),
        out_shape=jax.ShapeDtypeStruct((n_rows, w), hp.dtype),
        compiler_params=pltpu.CompilerParams(dimension_semantics=("arbitrary",),
                                             vmem_limit_bytes=VMEM_LIMIT, has_side_effects=True),
        name="moe_dispatch",
    )(dest, fill, hp)


def _experts_kernel(te_ref, nu_ref, x_ref, w1_ref, w3_ref, w2_ref, o_ref, *, tf):
    del te_ref
    i = pl.program_id(0)

    @pl.when(i < nu_ref[0])
    def _():
        o_ref[...] = _swiglu_chunks(
            x_ref[...].astype(BF16), lambda c0, n: w1_ref[0, :, c0:c0 + n],
            lambda c0, n: w3_ref[0, :, c0:c0 + n], lambda c0, n: w2_ref[0, c0:c0 + n, :],
            w1_ref.shape[2], tf)

    @pl.when(i >= nu_ref[0])
    def _():
        o_ref[...] = jnp.zeros_like(o_ref)


def _experts(xs, tile_expert, n_used, w1_bf, w3_bf, w2_bf, tm, tf):
    n_rows, w = xs.shape
    n_exp, d, ff = w1_bf.shape
    n_tiles = n_rows // tm

    def live(i, nu):
        return jnp.minimum(i, nu[0] - 1)

    once = pl.Buffered(1)
    return pl.pallas_call(
        functools.partial(_experts_kernel, tf=tf),
        grid_spec=pltpu.PrefetchScalarGridSpec(
            num_scalar_prefetch=2,
            grid=(n_tiles,),
            in_specs=[pl.BlockSpec((tm, w), lambda i, te, nu: (live(i, nu), 0)),
                      pl.BlockSpec((1, d, ff), lambda i, te, nu: (te[live(i, nu)], 0, 0), pipeline_mode=once),
                      pl.BlockSpec((1, d, ff), lambda i, te, nu: (te[live(i, nu)], 0, 0), pipeline_mode=once),
                      pl.BlockSpec((1, ff, d), lambda i, te, nu: (te[live(i, nu)], 0, 0), pipeline_mode=once)],
            out_specs=pl.BlockSpec((tm, d), lambda i, te, nu: (i, 0))),
        out_shape=jax.ShapeDtypeStruct((n_rows, d), F32),
        compiler_params=_cparams(("arbitrary",)),
        name="moe_experts",
    )(tile_expert, n_used, xs, w1_bf, w3_bf, w2_bf)


def _combine_kernel(pos_ref, x_ref, gate_ref, rt_ref, y_ref, o_ref, ybuf, sem, *, tc):
    i = pl.program_id(0)
    slot = i % 2

    def issue(tile, to_slot):
        base = tile * tc

        def body(r, _):
            for kk in range(TOP_K):
                src = pos_ref[(base + r) * TOP_K + kk]
                pltpu.make_async_copy(y_ref.at[pl.ds(src, 1)], ybuf.at[to_slot, kk, pl.ds(r, 1)],
                                      sem.at[to_slot]).start()
            return 0

        lax.fori_loop(0, tc, body, 0, unroll=8)

    @pl.when(i == 0)
    def _():
        issue(0, 0)

    @pl.when(i + 1 < pl.num_programs(0))
    def _():
        issue(i + 1, 1 - slot)

    for kk in range(TOP_K):
        pltpu.make_async_copy(y_ref.at[pl.ds(0, tc)], ybuf.at[slot, kk], sem.at[slot]).wait()
    rt = rt_ref[...]
    y = rt[:, 0:1] * ybuf[slot, 0] + rt[:, 1:2] * ybuf[slot, 1]
    o_ref[...] = x_ref[...] + gate_ref[0] * y


def _combine(x2, gate, route, y, pos, seq, tc):
    n, d = x2.shape
    return pl.pallas_call(
        functools.partial(_combine_kernel, tc=tc),
        grid_spec=pltpu.PrefetchScalarGridSpec(
            num_scalar_prefetch=1,
            grid=(n // tc,),
            in_specs=[pl.BlockSpec((tc, d), lambda i, p: (i, 0)),
                      pl.BlockSpec((1, 1, d), lambda i, p: ((i * tc) // seq, 0, 0)),
                      pl.BlockSpec((tc, LANES), lambda i, p: (i, 0)),
                      pl.BlockSpec(memory_space=pl.ANY)],
            out_specs=pl.BlockSpec((tc, d), lambda i, p: (i, 0)),
            scratch_shapes=[pltpu.VMEM((2, TOP_K, tc, d), F32), pltpu.SemaphoreType.DMA((2,))]),
        out_shape=jax.ShapeDtypeStruct((n, d), F32),
        compiler_params=_cparams(("arbitrary",)),
        name="moe_combine",
    )(pos, x2, gate, route, y)


def _route_positions(e_flat, n_exp, tm):
    a = e_flat.shape[0]
    onehot = (e_flat[:, None] == jnp.arange(n_exp, dtype=jnp.int32)[None, :]).astype(jnp.int32)
    csum = jnp.cumsum(onehot, axis=0)
    counts = csum[-1]
    padded = ((counts + tm - 1) // tm) * tm
    ends = jnp.cumsum(padded)
    starts = ends - padded
    dest = jnp.sum(onehot * (csum - 1 + starts[None, :]), axis=1).astype(jnp.int32)
    n_tiles = a // tm + n_exp
    tile_ids = jnp.arange(n_tiles, dtype=jnp.int32)
    tile_expert = jnp.minimum(jnp.sum((ends[None, :] // tm <= tile_ids[:, None]).astype(jnp.int32), axis=1),
                              n_exp - 1)
    n_used = (ends[-1] // tm).astype(jnp.int32).reshape(1)
    j = jnp.arange(tm, dtype=jnp.int32)[None, :]
    is_pad = counts[:, None] + j < padded[:, None]
    tail_rank = jnp.cumsum((~is_pad).astype(jnp.int32).reshape(-1)) - 1
    fill = jnp.where(is_pad.reshape(-1), (starts[:, None] + counts[:, None] + j).reshape(-1),
                     ends[-1] + tail_rank).astype(jnp.int32)
    return dest, fill, tile_expert.astype(jnp.int32), n_used, n_tiles * tm


def _blockdiag_groups(w, ngroups):
    nblk, bw, _ = w.shape
    per = nblk // ngroups
    eye = jnp.eye(per, dtype=w.dtype)
    wg = w.reshape(ngroups, per, bw, bw)
    full = wg[:, :, :, None, :] * eye[None, :, None, :, None]
    return full.reshape(ngroups, per * bw, per * bw)


def kernel(x, c, ctx, c_ctx, l0_w_mod, l0_b_mod, l0_norm1, l0_norm2, l0_w_qkv, l0_q_gain, l0_k_gain, l0_rpb, l0_w_o, l0_ffn_w1, l0_ffn_w3, l0_ffn_w2, l1_w_mod, l1_b_mod, l1_norm1, l1_norm2, l1_w_in, l1_conv_w, l1_conv_b, l1_gate_a_w, l1_gate_a_b, l1_gate_x_w, l1_gate_x_b, l1_lam, l1_w_out, l1_router_w, l1_router_b, l1_moe_w1, l1_moe_w3, l1_moe_w2):
    b, s, d = x.shape
    n_ctx = ctx.shape[1]
    head_dim = l0_q_gain.shape[0]
    n_heads = d // head_dim
    n_exp = l1_router_w.shape[1]
    tm = min(512, s)
    tmc = min(512, n_ctx)

    def mods(w_mod, b_mod):
        m = _adaln_mod(jnp.concatenate([c, c_ctx[None, :]], axis=0), w_mod, b_mod)
        lat = [m[:b, j * d:(j + 1) * d].reshape(b, 1, d) for j in range(6)]
        cx = [jnp.broadcast_to(m[b:b + 1, j * d:(j + 1) * d].reshape(1, 1, d), (b, 1, d)) for j in range(6)]
        return lat, cx

    ml, mc = mods(l0_w_mod, l0_b_mod)
    w_qkv = l0_w_qkv.astype(BF16)
    qg = (jnp.tile(l0_q_gain, n_heads) * (head_dim ** -0.5)).reshape(1, d)
    kg = jnp.tile(l0_k_gain, n_heads).reshape(1, d)
    ql, kl, vl = _qkv(x, l0_norm1, ml[0], ml[1], w_qkv, qg, kg, head_dim, tm)
    qc, kc, vc = _qkv(ctx, l0_norm1, mc[0], mc[1], w_qkv, qg, kg, head_dim, tmc)
    o_l = _attention(ql, kl, vl, kc, vc, _att_bias_table(l0_rpb))
    o_c = _ctx_attention(qc, kc, vc)
    w_o = l0_w_o.astype(BF16)
    x_l = _proj_residual(o_l, w_o, x, ml[2], tm)
    x_c = _proj_residual(o_c, w_o, ctx, mc[2], tmc)
    w1, w3, w2 = l0_ffn_w1.astype(BF16), l0_ffn_w3.astype(BF16), l0_ffn_w2.astype(BF16)
    tf0 = l0_ffn_w1.shape[1] // 2
    x_l = _ffn(x_l, l0_norm2, ml[3], ml[4], ml[5], w1, w3, w2, tm, tf0)
    x_c = _ffn(x_c, l0_norm2, mc[3], mc[4], mc[5], w1, w3, w2, tmc, tf0)

    ml, mc = mods(l1_w_mod, l1_b_mod)
    w_in = l1_w_in.astype(BF16)
    g_l, xr_l = _inproj(x_l, l1_norm1, ml[0], ml[1], w_in, tm)
    _, xr_c = _inproj(x_c, l1_norm1, mc[0], mc[1], w_in, tmc)
    dr = xr_l.shape[-1]
    ngroups = 4
    t_scan = min(256, n_ctx)
    zeros0 = jnp.zeros((b, 1, dr), F32)
    hs = []
    for di, rev in enumerate((False, True)):
        wa_g = _blockdiag_groups(l1_gate_a_w[di], ngroups).astype(BF16)
        wx_g = _blockdiag_groups(l1_gate_x_w[di], ngroups).astype(BF16)
        args = (l1_conv_w, l1_conv_b, wa_g, wx_g, l1_gate_a_b[di], l1_gate_x_b[di], l1_lam[di])
        h_c = _lru_scan(xr_c, *args, zeros0, rev, t_scan)
        h0 = h_c[:, 0:1, :] if rev else h_c[:, n_ctx - 1:n_ctx, :]
        hs.append(_lru_scan(xr_l, *args, h0, rev, t_scan))
    x_l = _lru_out(g_l, hs[0], hs[1], l1_w_out.astype(BF16), x_l, ml[2], tm)

    hp, route = _router(x_l, l1_norm2, ml[3], ml[4], l1_router_w, l1_router_b, tm)
    n = b * s
    route2 = route.reshape(n, LANES)
    e_flat = route2[:, 2:2 + TOP_K].astype(jnp.int32).reshape(n * TOP_K)
    tme = 512
    dest, fill, tile_expert, n_used, n_rows = _route_positions(e_flat, n_exp, tme)
    xs = _dispatch(hp.reshape(n, d), dest, fill, n_rows, min(512, n))
    y = _experts(xs, tile_expert, n_used, l1_moe_w1.astype(BF16), l1_moe_w3.astype(BF16),
                 l1_moe_w2.astype(BF16), tme, l1_moe_w1.shape[2] // 4)
    out = _combine(x_l.reshape(n, d), ml[5], route2, y, dest, s, min(256, n))
    return out.reshape(b, s, d)
```

```python
import functools

import numpy as np
import jax
import jax.numpy as jnp
from jax import lax
from jax.experimental import pallas as pl
from jax.experimental.pallas import tpu as pltpu

GRID_W = 64
WIN_H = 8
WIN_W = 16
TOP_K = 2
LRU_C = 8.0
EPS = 1e-6
CONV_LEFT = 2
NEG_BIG = -1e30
LANES = 128
SUBLANES = 8
VMEM_LIMIT = 56 * 1024 * 1024

ATT_ROWS = 8
ATT_KROWS = 16
ATT_PAIR = 2
ATT_BAND = 10

F32 = jnp.float32
BF16 = jnp.bfloat16


def _cparams(sem):
    return pltpu.CompilerParams(dimension_semantics=sem, vmem_limit_bytes=VMEM_LIMIT)


def _rms_mod(x, g, shift, scale):
    ms = jnp.mean(x * x, axis=-1, keepdims=True)
    y = x * lax.rsqrt(ms + EPS) * g
    return y * (1.0 + scale) + shift


def _mod_kernel(ct_ref, w_ref, b_ref, o_ref, *, nrows):
    ct = ct_ref[...]
    s = ct * jax.nn.sigmoid(ct)
    w = w_ref[...]
    rows = [jnp.sum(w * s[:, m:m + 1], axis=0, keepdims=True) + b_ref[...] for m in range(nrows)]
    rows += [jnp.zeros_like(rows[0])] * (SUBLANES - nrows)
    o_ref[...] = jnp.concatenate(rows, axis=0)


def _adaln_mod(cvecs, w_mod, b_mod):
    nrows, d = cvecs.shape
    n = w_mod.shape[1]
    tn = 768
    ct = jnp.zeros((d, SUBLANES), F32).at[:, :nrows].set(cvecs.T)
    return pl.pallas_call(
        functools.partial(_mod_kernel, nrows=nrows),
        grid=(n // tn,),
        in_specs=[pl.BlockSpec((d, SUBLANES), lambda j: (0, 0)),
                  pl.BlockSpec((d, tn), lambda j: (0, j)),
                  pl.BlockSpec((1, tn), lambda j: (0, j))],
        out_specs=pl.BlockSpec((SUBLANES, tn), lambda j: (0, j)),
        out_shape=jax.ShapeDtypeStruct((SUBLANES, n), F32),
        compiler_params=_cparams(("arbitrary",)),
        name="adaln_mod",
    )(ct, w_mod, b_mod.reshape(1, n))


def _qkv_kernel(x_ref, g_ref, sh_ref, sc_ref, w_ref, qg_ref, kg_ref, pm_ref, pe_ref,
                q_ref, k_ref, v_ref, *, d):
    h = _rms_mod(x_ref[0], g_ref[...], sh_ref[0], sc_ref[0]).astype(BF16)
    for part, (gain_ref, out_ref) in enumerate(((qg_ref, q_ref), (kg_ref, k_ref))):
        y = jnp.dot(h, w_ref[:, part * d:(part + 1) * d], preferred_element_type=F32)
        ms = jnp.dot((y * y).astype(BF16), pm_ref[...], preferred_element_type=F32)
        inv = lax.rsqrt(ms + EPS)
        inv_hi = inv.astype(BF16)
        inv_lo = (inv - inv_hi.astype(F32)).astype(BF16)
        full = (jnp.dot(inv_hi, pe_ref[...], preferred_element_type=F32)
                + jnp.dot(inv_lo, pe_ref[...], preferred_element_type=F32))
        out_ref[0] = (y * full * gain_ref[...]).astype(BF16)
    v_ref[0] = jnp.dot(h, w_ref[:, 2 * d:3 * d], preferred_element_type=F32).astype(BF16)


def _qkv(x, norm_g, shift, scale, w_qkv_bf, q_gain_full, k_gain_full, head_dim, tm):
    bx, sx, d = x.shape
    n_heads = d // head_dim
    head_of = np.arange(d) // head_dim
    pm = np.zeros((d, LANES), np.float32)
    pm[np.arange(d), head_of] = 1.0 / head_dim
    pe = np.zeros((LANES, d), np.float32)
    pe[head_of, np.arange(d)] = 1.0
    assert n_heads <= LANES
    row = lambda b, i: (b, i, 0)
    vec = lambda b, i: (b, 0, 0)
    const2 = lambda b, i: (0, 0)
    out = jax.ShapeDtypeStruct((bx, sx, d), BF16)
    return pl.pallas_call(
        functools.partial(_qkv_kernel, d=d),
        grid=(bx, sx // tm),
        in_specs=[pl.BlockSpec((1, tm, d), row),
                  pl.BlockSpec((1, d), const2),
                  pl.BlockSpec((1, 1, d), vec),
                  pl.BlockSpec((1, 1, d), vec),
                  pl.BlockSpec((d, 3 * d), const2),
                  pl.BlockSpec((1, d), const2),
                  pl.BlockSpec((1, d), const2),
                  pl.BlockSpec((d, LANES), const2),
                  pl.BlockSpec((LANES, d), const2)],
        out_specs=[pl.BlockSpec((1, tm, d), row)] * 3,
        out_shape=[out, out, out],
        compiler_params=_cparams(("parallel", "parallel")),
        name="qkv_headnorm",
    )(x, norm_g.reshape(1, d), shift, scale, w_qkv_bf, q_gain_full, k_gain_full,
      jnp.asarray(pm, BF16), jnp.asarray(pe, BF16))


def _att_band_layout(rows):
    npair = ATT_ROWS // ATT_PAIR
    half_win = WIN_H // 2
    sets = []
    for kind, off in (("mid", half_win), ("top", 0), ("bot", ATT_KROWS - ATT_ROWS)):
        for p in range(npair):
            if kind == "mid" and p > 0:
                continue
            rs_rel = []
            for a in range(ATT_PAIR):
                rq = ATT_PAIR * p + a
                if kind == "mid":
                    rs_rel.append(rq)
                elif kind == "top":
                    rs_rel.append(max(rq - half_win, 0))
                else:
                    rs_rel.append(min(rq + off - half_win, ATT_KROWS - WIN_H))
            m0 = min(rs_rel[0] // 2, (ATT_KROWS - ATT_BAND) // 2)
            sets.append((kind, p, off, rs_rel, m0))
    return sets


def _att_bias_table(rpb):
    n_heads, n_dr, n_dc = rpb.shape
    sets = _att_band_layout(None)
    cq = np.arange(GRID_W)
    cs = np.clip(cq - WIN_W // 2, 0, GRID_W - WIN_W)
    ck = np.arange(GRID_W)
    col_ok = (ck[None, :] >= cs[:, None]) & (ck[None, :] < cs[:, None] + WIN_W)
    dc = ck[None, :] - cq[:, None] + (WIN_W - 1)
    sel = ((dc[:, None, :] == np.arange(n_dc)[None, :, None]) & col_ok[:, None, :]).astype(np.float32)
    blocks = jnp.einsum("hrx,qxk->hrqk", rpb.astype(F32), jnp.asarray(sel), precision=lax.Precision.HIGHEST)
    blocks = jnp.where(jnp.asarray(col_ok)[None, None], blocks, NEG_BIG)
    blocks = jnp.concatenate([blocks, jnp.full((n_heads, 1, GRID_W, GRID_W), NEG_BIG, F32)], axis=1)
    which = np.full((len(sets), ATT_PAIR, ATT_BAND), n_dr, np.int32)
    for si, (kind, p, off, rs_rel, m0) in enumerate(sets):
        for a in range(ATT_PAIR):
            rq = ATT_PAIR * p + a if kind != "mid" else a
            rs = rs_rel[a] if kind != "mid" else a
            base = m0 if kind != "mid" else 0
            for j in range(ATT_BAND):
                rk = 2 * base + j
                if rs <= rk < rs + WIN_H:
                    which[si, a, j] = rk - rq - off + (WIN_H - 1)
    pairs = which.reshape(len(sets), ATT_PAIR, ATT_BAND // 2, 2)
    uniq, inv = np.unique(pairs.reshape(-1, 2), axis=0, return_inverse=True)
    lane_blocks = jnp.concatenate([jnp.take(blocks, jnp.asarray(uniq[:, 0]), axis=1),
                                   jnp.take(blocks, jnp.asarray(uniq[:, 1]), axis=1)], axis=-1)
    tab = jnp.take(lane_blocks, jnp.asarray(inv.reshape(-1)), axis=1)
    tab = tab.reshape(n_heads // 2, 2, len(sets), ATT_PAIR, ATT_BAND // 2, GRID_W, 2 * GRID_W)
    tab = tab.transpose(0, 2, 4, 1, 3, 5, 6)
    return tab.reshape(n_heads // 2, len(sets), ATT_BAND // 2, 2 * ATT_PAIR * GRID_W, 2 * GRID_W)


def _attn_kernel(q_ref, k0, k1, k2, k3, v0, v1, v2, v3, kc_ref, vc_ref, tb_ref, o_ref,
                 kwin, vwin, *, m0_top, m0_bot):
    i = pl.program_id(2)
    nt = pl.num_programs(2)
    ck = k0.shape[2]
    for c, (kr, vr) in enumerate(((k0, v0), (k1, v1), (k2, v2), (k3, v3))):
        kwin[c * ck:(c + 1) * ck, :] = kr[0, 0]
        vwin[c * ck:(c + 1) * ck, :] = vr[0, 0]
    kc = kc_ref[0]
    vc = vc_ref[0]
    lane = lax.broadcasted_iota(jnp.int32, (1, LANES), 1)
    head_lo = lane < (LANES // 2)
    nq = ATT_PAIR * GRID_W
    nk = ATT_BAND * GRID_W
    npair = ATT_ROWS // ATT_PAIR
    dn = (((1,), (1,)), ((), ()))
    q2s, ss, vbs = [], [], []
    for p in range(npair):
        m0 = jnp.where(i == 0, m0_top[p], jnp.where(i == nt - 1, m0_bot[p], p))
        st = jnp.where(i == 0, 1 + p, jnp.where(i == nt - 1, 1 + npair + p, 0))
        start = pl.multiple_of(m0 * LANES, LANES)
        kb = kwin[pl.ds(start, nk), :]
        vbs.append(vwin[pl.ds(start, nk), :])
        qp = q_ref[0, p * nq:(p + 1) * nq, :]
        zero = jnp.zeros_like(qp)
        q2 = jnp.concatenate([jnp.where(head_lo, qp, zero), jnp.where(head_lo, zero, qp)], axis=0)
        q2s.append(q2)
        bias = jnp.concatenate([tb_ref[0, st, lb] for lb in range(tb_ref.shape[2])], axis=1)
        ss.append(lax.dot_general(q2, kb, dn, preferred_element_type=F32) + bias)
    sc_all = lax.dot_general(jnp.concatenate(q2s, axis=0), kc, dn, preferred_element_type=F32)
    for p in range(npair):
        s = ss[p]
        vb = vbs[p]
        sc = sc_all[p * 2 * nq:(p + 1) * 2 * nq]
        m = jnp.maximum(jnp.max(s, axis=-1, keepdims=True), jnp.max(sc, axis=-1, keepdims=True))
        e = jnp.exp(s - m)
        ec = jnp.exp(sc - m)
        den = jnp.sum(e, axis=-1, keepdims=True) + jnp.sum(ec, axis=-1, keepdims=True)
        o2 = (jnp.dot(e.astype(BF16), vb, preferred_element_type=F32)
              + jnp.dot(ec.astype(BF16), vc, preferred_element_type=F32)) * (1.0 / den)
        o_ref[0, p * nq:(p + 1) * nq, :] = jnp.where(head_lo, o2[:nq], o2[nq:]).astype(BF16)


def _attention(q, k, v, kc, vc, bias_tab):
    b, s, d = q.shape
    c = kc.shape[1]
    rows = s // GRID_W
    assert rows % ATT_ROWS == 0 and rows >= ATT_KROWS
    tq = ATT_ROWS * GRID_W
    nt = s // tq
    chunk_rows = 4
    ck = chunk_rows * GRID_W
    nchunk = ATT_KROWS // chunk_rows
    k4 = k.reshape(b, s // ck, ck, d)
    v4 = v.reshape(b, s // ck, ck, d)
    sets = _att_band_layout(None)
    npair = ATT_ROWS // ATT_PAIR
    m0_top = tuple(sets[1 + p][4] for p in range(npair))
    m0_bot = tuple(sets[1 + npair + p][4] for p in range(npair))

    def kmap(t):
        def f(bi, j, i):
            c0 = jnp.clip(i * (ATT_ROWS // chunk_rows) - (WIN_H // 2) // chunk_rows, 0,
                          (rows - ATT_KROWS) // chunk_rows)
            return (bi, c0 + t, 0, j)
        return f

    kv_specs = [pl.BlockSpec((1, 1, ck, LANES), kmap(t)) for t in range(nchunk)]
    return pl.pallas_call(
        functools.partial(_attn_kernel, m0_top=m0_top, m0_bot=m0_bot),
        grid=(b, d // LANES, nt),
        in_specs=[pl.BlockSpec((1, tq, LANES), lambda bi, j, i: (bi, i, j))]
                 + kv_specs + kv_specs
                 + [pl.BlockSpec((1, c, LANES), lambda bi, j, i: (bi, 0, j)),
                    pl.BlockSpec((1, c, LANES), lambda bi, j, i: (bi, 0, j)),
                    pl.BlockSpec((1,) + bias_tab.shape[1:], lambda bi, j, i: (j, 0, 0, 0, 0))],
        out_specs=pl.BlockSpec((1, tq, LANES), lambda bi, j, i: (bi, i, j)),
        out_shape=jax.ShapeDtypeStruct((b, s, d), BF16),
        scratch_shapes=[pltpu.VMEM((ATT_KROWS * GRID_W, LANES), BF16),
                        pltpu.VMEM((ATT_KROWS * GRID_W, LANES), BF16)],
        compiler_params=_cparams(("parallel", "parallel", "arbitrary")),
        name="nbr_attention",
    )(q, *([k4] * nchunk), *([v4] * nchunk), kc, vc, bias_tab)


def _ctx_attn_kernel(q_ref, k_ref, v_ref, o_ref):
    lane = lax.broadcasted_iota(jnp.int32, (1, LANES), 1)
    head_lo = lane < (LANES // 2)
    q = q_ref[0]
    k = k_ref[0]
    v = v_ref[0]
    outs = []
    for hh in range(2):
        mask = head_lo if hh == 0 else jnp.logical_not(head_lo)
        qh = jnp.where(mask, q, jnp.zeros_like(q))
        s = lax.dot_general(qh, k, (((1,), (1,)), ((), ())), preferred_element_type=F32)
        m = jnp.max(s, axis=-1, keepdims=True)
        e = jnp.exp(s - m)
        den = jnp.sum(e, axis=-1, keepdims=True)
        o = jnp.dot(e.astype(BF16), v, preferred_element_type=F32)
        outs.append(o * (1.0 / den))
    o_ref[0] = jnp.where(head_lo, outs[0], outs[1]).astype(BF16)


def _ctx_attention(q, k, v):
    b, c, d = q.shape
    spec = pl.BlockSpec((1, c, LANES), lambda bi, j: (bi, 0, j))
    return pl.pallas_call(
        _ctx_attn_kernel,
        grid=(b, d // LANES),
        in_specs=[spec, spec, spec],
        out_specs=spec,
        out_shape=jax.ShapeDtypeStruct((b, c, d), BF16),
        compiler_params=_cparams(("parallel", "parallel")),
        name="ctx_attention",
    )(q, k, v)


def _proj_res_kernel(a_ref, w_ref, x_ref, gate_ref, o_ref):
    y = jnp.dot(a_ref[0], w_ref[...], preferred_element_type=F32)
    o_ref[0] = x_ref[0] + gate_ref[0] * y


def _proj_residual(a, w_bf, x, gate, tm):
    bx, sx, kdim = a.shape
    d = x.shape[-1]
    row = lambda b, i: (b, i, 0)
    return pl.pallas_call(
        _proj_res_kernel,
        grid=(bx, sx // tm),
        in_specs=[pl.BlockSpec((1, tm, kdim), row),
                  pl.BlockSpec((kdim, d), lambda b, i: (0, 0)),
                  pl.BlockSpec((1, tm, d), row),
                  pl.BlockSpec((1, 1, d), lambda b, i: (b, 0, 0))],
        out_specs=pl.BlockSpec((1, tm, d), row),
        out_shape=jax.ShapeDtypeStruct(x.shape, F32),
        compiler_params=_cparams(("parallel", "parallel")),
        name="proj_residual",
    )(a, w_bf, x, gate)


def _swiglu_chunks(h, w1_at, w3_at, w2_at, ff, tf):
    acc = None
    for c0 in range(0, ff, tf):
        a = jnp.dot(h, w1_at(c0, tf), preferred_element_type=F32)
        b = jnp.dot(h, w3_at(c0, tf), preferred_element_type=F32)
        t = (a * jax.nn.sigmoid(a) * b).astype(BF16)
        y = jnp.dot(t, w2_at(c0, tf), preferred_element_type=F32)
        acc = y if acc is None else acc + y
    return acc


def _ffn_kernel(x_ref, g_ref, sh_ref, sc_ref, gate_ref, w1_ref, w3_ref, w2_ref, o_ref, *, tf):
    x = x_ref[0]
    h = _rms_mod(x, g_ref[...], sh_ref[0], sc_ref[0]).astype(BF16)
    y = _swiglu_chunks(h, lambda c0, n: w1_ref[:, c0:c0 + n], lambda c0, n: w3_ref[:, c0:c0 + n],
                       lambda c0, n: w2_ref[c0:c0 + n, :], w1_ref.shape[1], tf)
    o_ref[0] = x + gate_ref[0] * y


def _ffn(x, norm_g, shift, scale, gate, w1_bf, w3_bf, w2_bf, tm, tf):
    bx, sx, d = x.shape
    ff = w1_bf.shape[1]
    row = lambda b, i: (b, i, 0)
    vec = lambda b, i: (b, 0, 0)
    const2 = lambda b, i: (0, 0)
    once = pl.Buffered(1)
    return pl.pallas_call(
        functools.partial(_ffn_kernel, tf=tf),
        grid=(bx, sx // tm),
        in_specs=[pl.BlockSpec((1, tm, d), row),
                  pl.BlockSpec((1, d), const2),
                  pl.BlockSpec((1, 1, d), vec),
                  pl.BlockSpec((1, 1, d), vec),
                  pl.BlockSpec((1, 1, d), vec),
                  pl.BlockSpec((d, ff), const2, pipeline_mode=once),
                  pl.BlockSpec((d, ff), const2, pipeline_mode=once),
                  pl.BlockSpec((ff, d), const2, pipeline_mode=once)],
        out_specs=pl.BlockSpec((1, tm, d), row),
        out_shape=jax.ShapeDtypeStruct(x.shape, F32),
        compiler_params=_cparams(("parallel", "parallel")),
        name="ffn_swiglu",
    )(x, norm_g.reshape(1, d), shift, scale, gate, w1_bf, w3_bf, w2_bf)


def _inproj_kernel(x_ref, xp_ref, xn_ref, g_ref, sh_ref, sc_ref, w_ref, cw_ref, cb_ref, gb_ref, xc_ref, ext,
                   *, dr):
    i = pl.program_id(1)
    nb = pl.num_programs(1)
    tm = x_ref.shape[1]
    pad = SUBLANES
    xe = jnp.concatenate([xp_ref[0], x_ref[0], xn_ref[0]], axis=0)
    h = _rms_mod(xe, g_ref[...], sh_ref[0], sc_ref[0]).astype(BF16)
    xr = jnp.dot(h, w_ref[:, dr:], preferred_element_type=F32)
    rowi = lax.broadcasted_iota(jnp.int32, (tm + 2 * pad, 1), 0)
    inside = jnp.logical_and(jnp.logical_or(rowi >= pad, i > 0),
                             jnp.logical_or(rowi < tm + pad, i < nb - 1))
    ext[...] = jnp.where(inside, xr, 0.0)
    gb_ref[0] = jnp.dot(h, w_ref[:, :dr], preferred_element_type=F32)[pad:pad + tm]
    cw = cw_ref[...]
    xc = cb_ref[...] + sum(
        cw[j:j + 1, :] * ext[pad + j - CONV_LEFT:pad + j - CONV_LEFT + tm, :] for j in range(cw.shape[0]))
    for k in range(dr // LANES):
        xc_ref[0, k] = xc[:, k * LANES:(k + 1) * LANES]


def _inproj(x, norm_g, shift, scale, w_in_bf, conv_w, conv_b, tm):
    bx, sx, d = x.shape
    dr = w_in_bf.shape[1] // 2
    tb = tm // SUBLANES
    row = lambda b, i: (b, i, 0)
    vec = lambda b, i: (b, 0, 0)
    const2 = lambda b, i: (0, 0)
    out = jax.ShapeDtypeStruct((bx, sx, dr), F32)
    return pl.pallas_call(
        functools.partial(_inproj_kernel, dr=dr),
        grid=(bx, sx // tm),
        in_specs=[pl.BlockSpec((1, tm, d), row),
                  pl.BlockSpec((1, SUBLANES, d), lambda b, i: (b, jnp.maximum(i * tb - 1, 0), 0)),
                  pl.BlockSpec((1, SUBLANES, d),
                               lambda b, i: (b, jnp.minimum((i + 1) * tb, sx // SUBLANES - 1), 0)),
                  pl.BlockSpec((1, d), const2),
                  pl.BlockSpec((1, 1, d), vec),
                  pl.BlockSpec((1, 1, d), vec),
                  pl.BlockSpec((d, 2 * dr), const2),
                  pl.BlockSpec(conv_w.shape, const2),
                  pl.BlockSpec((1, dr), const2)],
        out_specs=[pl.BlockSpec((1, tm, dr), row),
                   pl.BlockSpec((1, dr // LANES, tm, LANES), lambda b, i: (b, 0, i, 0))],
        out_shape=[out, jax.ShapeDtypeStruct((bx, dr // LANES, sx, LANES), F32)],
        scratch_shapes=[pltpu.VMEM((tm + 2 * SUBLANES, dr), F32)],
        compiler_params=_cparams(("parallel", "parallel")),
        name="rglru_inproj",
    )(x, x, x, norm_g.reshape(1, d), shift, scale, w_in_bf, conv_w, conv_b.reshape(1, dr))


def _lru_scan_kernel(xc_hbm, wa_ref, wx_ref, ba_ref, bx_ref, lam_ref, h0_ref, o_hbm, xbuf, obuf, carry,
                     sem_in, sem_out, *, reverse, ngroups):
    bi = pl.program_id(0)
    i = pl.program_id(1)
    nb = pl.num_programs(1)
    ncg, nl = xbuf.shape[1], xbuf.shape[2]
    t = nl * SUBLANES
    c = ncg * LANES
    slot = i % 2

    def block_start(step):
        return ((nb - 1 - step) if reverse else step) * t

    def in_copies(step, to_slot):
        return [pltpu.make_async_copy(xc_hbm.at[bi, :, pl.ds(block_start(step) + ch * nl, nl), :],
                                      xbuf.at[to_slot, :, :, ch, :], sem_in.at[to_slot])
                for ch in range(SUBLANES)]

    def out_copies(step, from_slot):
        return [pltpu.make_async_copy(obuf.at[from_slot, :, :, ch, :],
                                      o_hbm.at[bi, :, pl.ds(block_start(step) + ch * nl, nl), :],
                                      sem_out.at[from_slot])
                for ch in range(SUBLANES)]

    @pl.when(i == 0)
    def _():
        carry[...] = jnp.broadcast_to(h0_ref[0], carry.shape)
        for cp in in_copies(0, 0):
            cp.start()

    @pl.when(i + 1 < nb)
    def _():
        for cp in in_copies(i + 1, 1 - slot):
            cp.start()

    for cp in in_copies(i, slot):
        cp.wait()

    @pl.when(i >= 2)
    def _():
        for cp in out_copies(i, slot):
            cp.wait()

    gw = c // ngroups
    kpg = gw // LANES
    steps = range(nl - 1, -1, -1) if reverse else range(nl)
    rows = lambda v, j: v[j * SUBLANES:(j + 1) * SUBLANES]
    sub = lax.broadcasted_iota(jnp.int32, (SUBLANES, 1), 0)
    for gi in range(ngroups):
        cs = slice(gi * gw, (gi + 1) * gw)
        xc = jnp.concatenate([xbuf[slot, gi * kpg + k].reshape(t, LANES) for k in range(kpg)], axis=1)
        xb = xc.astype(BF16)
        t_a = jnp.tanh(jnp.dot(xb, wa_ref[gi], preferred_element_type=F32) + 0.5 * ba_ref[:, cs])
        t_x = jnp.tanh(jnp.dot(xb, wx_ref[gi], preferred_element_type=F32) + 0.5 * bx_ref[:, cs])
        lam = lam_ref[:, cs]
        sp = jnp.maximum(-lam, 0.0) + jnp.log(1.0 + jnp.exp(-jnp.abs(lam)))
        k1 = ((-0.5 * LRU_C) * np.float32(np.log2(np.e))) * sp
        a = jnp.exp2(t_a * k1 + k1)
        om = 1.0 - a * a
        u = (om * lax.rsqrt(jnp.maximum(om, 1e-37))) * ((0.5 * t_x + 0.5) * xc)

        hloc = jnp.zeros((SUBLANES, gw), F32)
        aprod = jnp.ones((SUBLANES, gw), F32)
        for j in steps:
            hloc = rows(a, j) * hloc + rows(u, j)
            aprod = rows(a, j) * aprod

        hprev = carry[:, cs]
        for sft in (1, 2, 4):
            if reverse:
                a_sh = pltpu.roll(aprod, SUBLANES - sft, axis=0)
                h_sh = pltpu.roll(hloc, SUBLANES - sft, axis=0)
                ok = sub < SUBLANES - sft
            else:
                a_sh = pltpu.roll(aprod, sft, axis=0)
                h_sh = pltpu.roll(hloc, sft, axis=0)
                ok = sub >= sft
            hloc = jnp.where(ok, aprod * h_sh + hloc, hloc)
            aprod = jnp.where(ok, aprod * a_sh, aprod)
        after = hloc + aprod * hprev
        if reverse:
            h = jnp.where(sub == SUBLANES - 1, hprev, pltpu.roll(after, SUBLANES - 1, axis=0))
            carry[:, cs] = jnp.broadcast_to(after[0:1], after.shape)
        else:
            h = jnp.where(sub == 0, hprev, pltpu.roll(after, 1, axis=0))
            carry[:, cs] = jnp.broadcast_to(after[SUBLANES - 1:SUBLANES], after.shape)

        for j in steps:
            h = rows(a, j) * h + rows(u, j)
            for k in range(kpg):
                obuf[slot, gi * kpg + k, j] = h[:, k * LANES:(k + 1) * LANES]
    for cp in out_copies(i, slot):
        cp.start()

    @pl.when(i == nb - 1)
    def _():
        for cp in out_copies(i, slot):
            cp.wait()

    @pl.when(jnp.logical_and(i == nb - 1, nb >= 2))
    def _():
        for cp in out_copies(i, 1 - slot):
            cp.wait()


def _lru_scan(xc, wa_g, wx_g, ba, bx, lam, h0, reverse, t):
    b, ncg, s, _ = xc.shape
    c = ncg * LANES
    nb = s // t
    nl = t // SUBLANES
    ngroups = wa_g.shape[0]
    gw = c // ngroups
    const2 = lambda bi, i: (0, 0)
    const3 = lambda bi, i: (0, 0, 0)
    return pl.pallas_call(
        functools.partial(_lru_scan_kernel, reverse=reverse, ngroups=ngroups),
        grid=(b, nb),
        in_specs=[pl.BlockSpec(memory_space=pl.ANY),
                  pl.BlockSpec((ngroups, gw, gw), const3),
                  pl.BlockSpec((ngroups, gw, gw), const3),
                  pl.BlockSpec((1, c), const2),
                  pl.BlockSpec((1, c), const2),
                  pl.BlockSpec((1, c), const2),
                  pl.BlockSpec((1, 1, c), lambda bi, i: (bi, 0, 0))],
        out_specs=pl.BlockSpec(memory_space=pl.ANY),
        out_shape=jax.ShapeDtypeStruct((b, ncg, s, LANES), F32),
        scratch_shapes=[pltpu.VMEM((2, ncg, nl, SUBLANES, LANES), F32),
                        pltpu.VMEM((2, ncg, nl, SUBLANES, LANES), F32),
                        pltpu.VMEM((SUBLANES, c), F32),
                        pltpu.SemaphoreType.DMA((2,)),
                        pltpu.SemaphoreType.DMA((2,))],
        compiler_params=_cparams(("parallel", "arbitrary")),
        name="lru_scan_bwd" if reverse else "lru_scan_fwd",
    )(xc, wa_g, wx_g, ba.reshape(1, c), bx.reshape(1, c), lam.reshape(1, c), h0)


def _lru_out_kernel(g_ref, hf_ref, hb_ref, w_ref, x_ref, gate_ref, o_ref):
    hs = jnp.concatenate([hf_ref[0, k] + hb_ref[0, k] for k in range(hf_ref.shape[1])], axis=1)
    a = (jax.nn.gelu(g_ref[0]) * hs).astype(BF16)
    y = jnp.dot(a, w_ref[...], preferred_element_type=F32)
    o_ref[0] = x_ref[0] + gate_ref[0] * y


def _lru_out(g, hf, hb, w_out_bf, x, gate, tm):
    bx, sx, dr = g.shape
    d = x.shape[-1]
    row = lambda b, i: (b, i, 0)
    cg = pl.BlockSpec((1, dr // LANES, tm, LANES), lambda b, i: (b, 0, i, 0))
    return pl.pallas_call(
        _lru_out_kernel,
        grid=(bx, sx // tm),
        in_specs=[pl.BlockSpec((1, tm, dr), row), cg, cg]
                 + [pl.BlockSpec((dr, d), lambda b, i: (0, 0)),
                    pl.BlockSpec((1, tm, d), row),
                    pl.BlockSpec((1, 1, d), lambda b, i: (b, 0, 0))],
        out_specs=pl.BlockSpec((1, tm, d), row),
        out_shape=jax.ShapeDtypeStruct(x.shape, F32),
        compiler_params=_cparams(("parallel", "parallel")),
        name="rglru_outproj",
    )(g, hf, hb, w_out_bf, x, gate)


def _router_kernel(x_ref, g_ref, sh_ref, sc_ref, wh_ref, wl_ref, br_ref, hp_ref, rt_ref, *, n_exp):
    h = _rms_mod(x_ref[0], g_ref[...], sh_ref[0], sc_ref[0])
    hb = h.astype(BF16)
    hf = hb.astype(F32)
    h_lo = (h - hf).astype(BF16)
    logits = (jnp.dot(hb, wh_ref[...], preferred_element_type=F32)
              + jnp.dot(hb, wl_ref[...], preferred_element_type=F32)
              + jnp.dot(h_lo, wh_ref[...], preferred_element_type=F32)) + br_ref[...]
    lane = lax.broadcasted_iota(jnp.int32, logits.shape, 1)
    logits = jnp.where(lane < n_exp, logits, -jnp.inf)
    m1 = jnp.max(logits, axis=-1, keepdims=True)
    i1 = jnp.min(jnp.where(logits == m1, lane, LANES), axis=-1, keepdims=True)
    rest = jnp.where(lane == i1, -jnp.inf, logits)
    m2 = jnp.max(rest, axis=-1, keepdims=True)
    i2 = jnp.min(jnp.where(rest == m2, lane, LANES), axis=-1, keepdims=True)
    e2 = jnp.exp(m2 - m1)
    g1 = 1.0 / (1.0 + e2)
    g2 = e2 * g1
    rt_ref[0] = jnp.where(lane == 0, g1, jnp.where(lane == 1, g2, jnp.where(
        lane == 2, i1.astype(F32), jnp.where(lane == 3, i2.astype(F32), 0.0))))
    hp_ref[0] = h


def _router(x, norm_g, shift, scale, w_router, b_router, tm):
    bx, sx, d = x.shape
    n_exp = w_router.shape[1]
    wpad = jnp.zeros((d, LANES), F32).at[:, :n_exp].set(w_router)
    w_hi = wpad.astype(BF16)
    w_lo = (wpad - w_hi.astype(F32)).astype(BF16)
    bpad = jnp.zeros((1, LANES), F32).at[0, :n_exp].set(b_router)
    row = lambda b, i: (b, i, 0)
    vec = lambda b, i: (b, 0, 0)
    const2 = lambda b, i: (0, 0)
    return pl.pallas_call(
        functools.partial(_router_kernel, n_exp=n_exp),
        grid=(bx, sx // tm),
        in_specs=[pl.BlockSpec((1, tm, d), row),
                  pl.BlockSpec((1, d), const2),
                  pl.BlockSpec((1, 1, d), vec),
                  pl.BlockSpec((1, 1, d), vec),
                  pl.BlockSpec((d, LANES), const2),
                  pl.BlockSpec((d, LANES), const2),
                  pl.BlockSpec((1, LANES), const2)],
        out_specs=[pl.BlockSpec((1, tm, d), row), pl.BlockSpec((1, tm, LANES), row)],
        out_shape=[jax.ShapeDtypeStruct((bx, sx, d), F32),
                   jax.ShapeDtypeStruct((bx, sx, LANES), F32)],
        compiler_params=_cparams(("parallel", "parallel")),
        name="moe_router",
    )(x, norm_g.reshape(1, d), shift, scale, w_hi, w_lo, bpad)


def _dispatch_kernel(dest_ref, fill_ref, hp_ref, xs_ref, zrow, sem, zsem, *, td, n_fill):
    base = pl.program_id(0) * td

    @pl.when(pl.program_id(0) == 0)
    def _():
        zrow[...] = jnp.zeros_like(zrow)

        def zissue(r, _):
            pltpu.make_async_copy(zrow.at[pl.ds(0, 1)], xs_ref.at[pl.ds(fill_ref[r], 1)], zsem).start()
            return 0

        lax.fori_loop(0, n_fill, zissue, 0, unroll=8)

    def issue(r, _):
        for kk in range(TOP_K):
            dst = dest_ref[(base + r) * TOP_K + kk]
            pltpu.make_async_copy(hp_ref.at[pl.ds(r, 1)], xs_ref.at[pl.ds(dst, 1)], sem).start()
        return 0

    lax.fori_loop(0, td, issue, 0, unroll=8)
    for kk in range(TOP_K):
        pltpu.make_async_copy(hp_ref, xs_ref.at[pl.ds(0, td)], sem).wait()

    @pl.when(pl.program_id(0) == 0)
    def _():
        pltpu.make_async_copy(xs_ref.at[pl.ds(0, n_fill)], xs_ref.at[pl.ds(0, n_fill)], zsem).wait()


def _dispatch(hp, dest, fill, n_rows, td):
    n, w = hp.shape
    n_fill = fill.shape[0]
    return pl.pallas_call(
        functools.partial(_dispatch_kernel, td=td, n_fill=n_fill),
        grid_spec=pltpu.PrefetchScalarGridSpec(
            num_scalar_prefetch=2,
            grid=(n // td,),
            in_specs=[pl.BlockSpec((td, w), lambda i, dref, fref: (i, 0))],
            out_specs=pl.BlockSpec(memory_space=pl.ANY),
            scratch_shapes=[pltpu.VMEM((SUBLANES, w), hp.dtype),
                            pltpu.SemaphoreType.DMA(()), pltpu.SemaphoreType.DMA(())]),
        out_shape=jax.ShapeDtypeStruct((n_rows, w), hp.dtype),
        compiler_params=pltpu.CompilerParams(dimension_semantics=("arbitrary",),
                                             vmem_limit_bytes=VMEM_LIMIT, has_side_effects=True),
        name="moe_dispatch",
    )(dest, fill, hp)


def _experts_kernel(te_ref, nu_ref, x_ref, w1_ref, w3_ref, w2_ref, o_ref, *, tf):
    del te_ref
    i = pl.program_id(0)

    @pl.when(i < nu_ref[0])
    def _():
        o_ref[...] = _swiglu_chunks(
            x_ref[...].astype(BF16), lambda c0, n: w1_ref[0, :, c0:c0 + n],
            lambda c0, n: w3_ref[0, :, c0:c0 + n], lambda c0, n: w2_ref[0, c0:c0 + n, :],
            w1_ref.shape[2], tf)

    @pl.when(i >= nu_ref[0])
    def _():
        o_ref[...] = jnp.zeros_like(o_ref)


def _experts(xs, tile_expert, n_used, w1_bf, w3_bf, w2_bf, tm, tf):
    n_rows, w = xs.shape
    n_exp, d, ff = w1_bf.shape
    n_tiles = n_rows // tm

    def live(i, nu):
        return jnp.minimum(i, nu[0] - 1)

    once = pl.Buffered(1)
    return pl.pallas_call(
        functools.partial(_experts_kernel, tf=tf),
        grid_spec=pltpu.PrefetchScalarGridSpec(
            num_scalar_prefetch=2,
            grid=(n_tiles,),
            in_specs=[pl.BlockSpec((tm, w), lambda i, te, nu: (live(i, nu), 0)),
                      pl.BlockSpec((1, d, ff), lambda i, te, nu: (te[live(i, nu)], 0, 0), pipeline_mode=once),
                      pl.BlockSpec((1, d, ff), lambda i, te, nu: (te[live(i, nu)], 0, 0), pipeline_mode=once),
                      pl.BlockSpec((1, ff, d), lambda i, te, nu: (te[live(i, nu)], 0, 0), pipeline_mode=once)],
            out_specs=pl.BlockSpec((tm, d), lambda i, te, nu: (i, 0))),
        out_shape=jax.ShapeDtypeStruct((n_rows, d), F32),
        compiler_params=_cparams(("arbitrary",)),
        name="moe_experts",
    )(tile_expert, n_used, xs, w1_bf, w3_bf, w2_bf)


def _combine_kernel(pos_ref, x_ref, gate_ref, rt_ref, y_ref, o_ref, ybuf, sem, *, tc):
    i = pl.program_id(0)
    slot = i % 2

    def issue(tile, to_slot):
        base = tile * tc

        def body(r, _):
            for kk in range(TOP_K):
                src = pos_ref[(base + r) * TOP_K + kk]
                pltpu.make_async_copy(y_ref.at[pl.ds(src, 1)], ybuf.at[to_slot, kk, pl.ds(r, 1)],
                                      sem.at[to_slot]).start()
            return 0

        lax.fori_loop(0, tc, body, 0, unroll=8)

    @pl.when(i == 0)
    def _():
        issue(0, 0)

    @pl.when(i + 1 < pl.num_programs(0))
    def _():
        issue(i + 1, 1 - slot)

    for kk in range(TOP_K):
        pltpu.make_async_copy(y_ref.at[pl.ds(0, tc)], ybuf.at[slot, kk], sem.at[slot]).wait()
    rt = rt_ref[...]
    y = rt[:, 0:1] * ybuf[slot, 0] + rt[:, 1:2] * ybuf[slot, 1]
    o_ref[...] = x_ref[...] + gate_ref[0] * y


def _combine(x2, gate, route, y, pos, seq, tc):
    n, d = x2.shape
    return pl.pallas_call(
        functools.partial(_combine_kernel, tc=tc),
        grid_spec=pltpu.PrefetchScalarGridSpec(
            num_scalar_prefetch=1,
            grid=(n // tc,),
            in_specs=[pl.BlockSpec((tc, d), lambda i, p: (i, 0)),
                      pl.BlockSpec((1, 1, d), lambda i, p: ((i * tc) // seq, 0, 0)),
                      pl.BlockSpec((tc, LANES), lambda i, p: (i, 0)),
                      pl.BlockSpec(memory_space=pl.ANY)],
            out_specs=pl.BlockSpec((tc, d), lambda i, p: (i, 0)),
            scratch_shapes=[pltpu.VMEM((2, TOP_K, tc, d), F32), pltpu.SemaphoreType.DMA((2,))]),
        out_shape=jax.ShapeDtypeStruct((n, d), F32),
        compiler_params=_cparams(("arbitrary",)),
        name="moe_combine",
    )(pos, x2, gate, route, y)


def _route_positions(e_flat, n_exp, tm):
    a = e_flat.shape[0]
    onehot = (e_flat[:, None] == jnp.arange(n_exp, dtype=jnp.int32)[None, :]).astype(jnp.int32)
    csum = jnp.cumsum(onehot, axis=0)
    counts = csum[-1]
    padded = ((counts + tm - 1) // tm) * tm
    ends = jnp.cumsum(padded)
    starts = ends - padded
    dest = jnp.sum(onehot * (csum - 1 + starts[None, :]), axis=1).astype(jnp.int32)
    n_tiles = a // tm + n_exp
    tile_ids = jnp.arange(n_tiles, dtype=jnp.int32)
    tile_expert = jnp.minimum(jnp.sum((ends[None, :] // tm <= tile_ids[:, None]).astype(jnp.int32), axis=1),
                              n_exp - 1)
    n_used = (ends[-1] // tm).astype(jnp.int32).reshape(1)
    j = jnp.arange(tm, dtype=jnp.int32)[None, :]
    is_pad = counts[:, None] + j < padded[:, None]
    tail_rank = jnp.cumsum((~is_pad).astype(jnp.int32).reshape(-1)) - 1
    fill = jnp.where(is_pad.reshape(-1), (starts[:, None] + counts[:, None] + j).reshape(-1),
                     ends[-1] + tail_rank).astype(jnp.int32)
    return dest, fill, tile_expert.astype(jnp.int32), n_used, n_tiles * tm


def _blockdiag_groups(w, ngroups):
    nblk, bw, _ = w.shape
    per = nblk // ngroups
    eye = jnp.eye(per, dtype=w.dtype)
    wg = w.reshape(ngroups, per, bw, bw)
    full = wg[:, :, :, None, :] * eye[None, :, None, :, None]
    return full.reshape(ngroups, per * bw, per * bw)


def kernel(x, c, ctx, c_ctx, l0_w_mod, l0_b_mod, l0_norm1, l0_norm2, l0_w_qkv, l0_q_gain, l0_k_gain, l0_rpb, l0_w_o, l0_ffn_w1, l0_ffn_w3, l0_ffn_w2, l1_w_mod, l1_b_mod, l1_norm1, l1_norm2, l1_w_in, l1_conv_w, l1_conv_b, l1_gate_a_w, l1_gate_a_b, l1_gate_x_w, l1_gate_x_b, l1_lam, l1_w_out, l1_router_w, l1_router_b, l1_moe_w1, l1_moe_w3, l1_moe_w2):
    b, s, d = x.shape
    n_ctx = ctx.shape[1]
    head_dim = l0_q_gain.shape[0]
    n_heads = d // head_dim
    n_exp = l1_router_w.shape[1]
    tm = min(512, s)
    tmc = min(512, n_ctx)

    def mods(w_mod, b_mod):
        m = _adaln_mod(jnp.concatenate([c, c_ctx[None, :]], axis=0), w_mod, b_mod)
        lat = [m[:b, j * d:(j + 1) * d].reshape(b, 1, d) for j in range(6)]
        cx = [jnp.broadcast_to(m[b:b + 1, j * d:(j + 1) * d].reshape(1, 1, d), (b, 1, d)) for j in range(6)]
        return lat, cx

    ml, mc = mods(l0_w_mod, l0_b_mod)
    w_qkv = l0_w_qkv.astype(BF16)
    qg = (jnp.tile(l0_q_gain, n_heads) * (head_dim ** -0.5)).reshape(1, d)
    kg = jnp.tile(l0_k_gain, n_heads).reshape(1, d)
    ql, kl, vl = _qkv(x, l0_norm1, ml[0], ml[1], w_qkv, qg, kg, head_dim, tm)
    qc, kc, vc = _qkv(ctx, l0_norm1, mc[0], mc[1], w_qkv, qg, kg, head_dim, tmc)
    o_l = _attention(ql, kl, vl, kc, vc, _att_bias_table(l0_rpb))
    o_c = _ctx_attention(qc, kc, vc)
    w_o = l0_w_o.astype(BF16)
    x_l = _proj_residual(o_l, w_o, x, ml[2], tm)
    x_c = _proj_residual(o_c, w_o, ctx, mc[2], tmc)
    w1, w3, w2 = l0_ffn_w1.astype(BF16), l0_ffn_w3.astype(BF16), l0_ffn_w2.astype(BF16)
    tf0 = l0_ffn_w1.shape[1]
    x_l = _ffn(x_l, l0_norm2, ml[3], ml[4], ml[5], w1, w3, w2, tm, tf0)
    x_c = _ffn(x_c, l0_norm2, mc[3], mc[4], mc[5], w1, w3, w2, tmc, tf0)

    ml, mc = mods(l1_w_mod, l1_b_mod)
    w_in = l1_w_in.astype(BF16)
    g_l, xr_l = _inproj(x_l, l1_norm1, ml[0], ml[1], w_in, l1_conv_w, l1_conv_b, tm)
    _, xr_c = _inproj(x_c, l1_norm1, mc[0], mc[1], w_in, l1_conv_w, l1_conv_b, tmc)
    dr = g_l.shape[-1]
    ngroups = 4
    t_scan = min(256, n_ctx)
    zeros0 = jnp.zeros((b, 1, dr), F32)
    hs = []
    for di, rev in enumerate((False, True)):
        wa_g = (0.5 * _blockdiag_groups(l1_gate_a_w[di], ngroups)).astype(BF16)
        wx_g = (0.5 * _blockdiag_groups(l1_gate_x_w[di], ngroups)).astype(BF16)
        args = (wa_g, wx_g, l1_gate_a_b[di], l1_gate_x_b[di], l1_lam[di])
        h_c = _lru_scan(xr_c, *args, zeros0, rev, t_scan)
        h0 = (h_c[:, :, 0, :] if rev else h_c[:, :, n_ctx - 1, :]).reshape(b, 1, dr)
        hs.append(_lru_scan(xr_l, *args, h0, rev, t_scan))
    x_l = _lru_out(g_l, hs[0], hs[1], l1_w_out.astype(BF16), x_l, ml[2], tm)

    hp, route = _router(x_l, l1_norm2, ml[3], ml[4], l1_router_w, l1_router_b, tm)
    n = b * s
    route2 = route.reshape(n, LANES)
    e_flat = route2[:, 2:2 + TOP_K].astype(jnp.int32).reshape(n * TOP_K)
    tme = 512
    dest, fill, tile_expert, n_used, n_rows = _route_positions(e_flat, n_exp, tme)
    xs = _dispatch(hp.reshape(n, d), dest, fill, n_rows, min(512, n))
    y = _experts(xs, tile_expert, n_used, l1_moe_w1.astype(BF16), l1_moe_w3.astype(BF16),
                 l1_moe_w2.astype(BF16), tme, l1_moe_w1.shape[2] // 2)
    out = _combine(x_l.reshape(n, d), ml[5], route2, y, dest, s, min(256, n))
    return out.reshape(b, s, d)
```

```python
import functools

import numpy as np
import jax
import jax.numpy as jnp
from jax import lax
from jax.experimental import pallas as pl
from jax.experimental.pallas import tpu as pltpu

GRID_W = 64
WIN_H = 8
WIN_W = 16
TOP_K = 2
LRU_C = 8.0
EPS = 1e-6
CONV_LEFT = 2
NEG_BIG = -1e30
LANES = 128
SUBLANES = 8
VMEM_LIMIT = 56 * 1024 * 1024

ATT_ROWS = 8
ATT_KROWS = 16
ATT_PAIR = 2
ATT_BAND = 10
ATT_LOGIT_BOUND = 40.0

F32 = jnp.float32
BF16 = jnp.bfloat16


def _cparams(sem):
    return pltpu.CompilerParams(dimension_semantics=sem, vmem_limit_bytes=VMEM_LIMIT)


def _rms_mod(x, g, shift, scale):
    ms = jnp.mean(x * x, axis=-1, keepdims=True)
    y = x * lax.rsqrt(ms + EPS) * g
    return y * (1.0 + scale) + shift


def _mod_kernel(ct_ref, w_ref, b_ref, o_ref, *, nrows):
    ct = ct_ref[...]
    s = ct * jax.nn.sigmoid(ct)
    w = w_ref[...]
    rows = [jnp.sum(w * s[:, m:m + 1], axis=0, keepdims=True) + b_ref[...] for m in range(nrows)]
    rows += [jnp.zeros_like(rows[0])] * (SUBLANES - nrows)
    o_ref[...] = jnp.concatenate(rows, axis=0)


def _adaln_mod(cvecs, w_mod, b_mod):
    nrows, d = cvecs.shape
    n = w_mod.shape[1]
    tn = 768
    ct = jnp.zeros((d, SUBLANES), F32).at[:, :nrows].set(cvecs.T)
    return pl.pallas_call(
        functools.partial(_mod_kernel, nrows=nrows),
        grid=(n // tn,),
        in_specs=[pl.BlockSpec((d, SUBLANES), lambda j: (0, 0)),
                  pl.BlockSpec((d, tn), lambda j: (0, j)),
                  pl.BlockSpec((1, tn), lambda j: (0, j))],
        out_specs=pl.BlockSpec((SUBLANES, tn), lambda j: (0, j)),
        out_shape=jax.ShapeDtypeStruct((SUBLANES, n), F32),
        compiler_params=_cparams(("arbitrary",)),
        name="adaln_mod",
    )(ct, w_mod, b_mod.reshape(1, n))


def _qkv_kernel(x_ref, g_ref, sh_ref, sc_ref, w_ref, qg_ref, kg_ref, pm_ref, pe_ref,
                q_ref, k_ref, v_ref, *, d):
    h = _rms_mod(x_ref[0], g_ref[...], sh_ref[0], sc_ref[0]).astype(BF16)
    for part, (gain_ref, out_ref) in enumerate(((qg_ref, q_ref), (kg_ref, k_ref))):
        y = jnp.dot(h, w_ref[:, part * d:(part + 1) * d], preferred_element_type=F32)
        ms = jnp.dot((y * y).astype(BF16), pm_ref[...], preferred_element_type=F32)
        inv = lax.rsqrt(ms + EPS)
        inv_hi = inv.astype(BF16)
        inv_lo = (inv - inv_hi.astype(F32)).astype(BF16)
        full = (jnp.dot(inv_hi, pe_ref[...], preferred_element_type=F32)
                + jnp.dot(inv_lo, pe_ref[...], preferred_element_type=F32))
        out_ref[0] = (y * full * gain_ref[...]).astype(BF16)
    v_ref[0] = jnp.dot(h, w_ref[:, 2 * d:3 * d], preferred_element_type=F32).astype(BF16)


def _qkv(x, norm_g, shift, scale, w_qkv_bf, q_gain_full, k_gain_full, head_dim, tm):
    bx, sx, d = x.shape
    n_heads = d // head_dim
    head_of = np.arange(d) // head_dim
    pm = np.zeros((d, LANES), np.float32)
    pm[np.arange(d), head_of] = 1.0 / head_dim
    pe = np.zeros((LANES, d), np.float32)
    pe[head_of, np.arange(d)] = 1.0
    assert n_heads <= LANES
    row = lambda b, i: (b, i, 0)
    vec = lambda b, i: (b, 0, 0)
    const2 = lambda b, i: (0, 0)
    out = jax.ShapeDtypeStruct((bx, sx, d), BF16)
    return pl.pallas_call(
        functools.partial(_qkv_kernel, d=d),
        grid=(bx, sx // tm),
        in_specs=[pl.BlockSpec((1, tm, d), row),
                  pl.BlockSpec((1, d), const2),
                  pl.BlockSpec((1, 1, d), vec),
                  pl.BlockSpec((1, 1, d), vec),
                  pl.BlockSpec((d, 3 * d), const2),
                  pl.BlockSpec((1, d), const2),
                  pl.BlockSpec((1, d), const2),
                  pl.BlockSpec((d, LANES), const2),
                  pl.BlockSpec((LANES, d), const2)],
        out_specs=[pl.BlockSpec((1, tm, d), row)] * 3,
        out_shape=[out, out, out],
        compiler_params=_cparams(("parallel", "parallel")),
        name="qkv_headnorm",
    )(x, norm_g.reshape(1, d), shift, scale, w_qkv_bf, q_gain_full, k_gain_full,
      jnp.asarray(pm, BF16), jnp.asarray(pe, BF16))


def _att_band_layout(rows):
    npair = ATT_ROWS // ATT_PAIR
    half_win = WIN_H // 2
    sets = []
    for kind, off in (("mid", half_win), ("top", 0), ("bot", ATT_KROWS - ATT_ROWS)):
        for p in range(npair):
            if kind == "mid" and p > 0:
                continue
            rs_rel = []
            for a in range(ATT_PAIR):
                rq = ATT_PAIR * p + a
                if kind == "mid":
                    rs_rel.append(rq)
                elif kind == "top":
                    rs_rel.append(max(rq - half_win, 0))
                else:
                    rs_rel.append(min(rq + off - half_win, ATT_KROWS - WIN_H))
            m0 = min(rs_rel[0] // 2, (ATT_KROWS - ATT_BAND) // 2)
            sets.append((kind, p, off, rs_rel, m0))
    return sets


def _att_bias_table(rpb):
    n_heads, n_dr, n_dc = rpb.shape
    sets = _att_band_layout(None)
    cq = np.arange(GRID_W)
    cs = np.clip(cq - WIN_W // 2, 0, GRID_W - WIN_W)
    ck = np.arange(GRID_W)
    col_ok = (ck[None, :] >= cs[:, None]) & (ck[None, :] < cs[:, None] + WIN_W)
    dc = ck[None, :] - cq[:, None] + (WIN_W - 1)
    sel = ((dc[:, None, :] == np.arange(n_dc)[None, :, None]) & col_ok[:, None, :]).astype(np.float32)
    blocks = jnp.einsum("hrx,qxk->hrqk", rpb.astype(F32), jnp.asarray(sel), precision=lax.Precision.HIGHEST)
    blocks = jnp.where(jnp.asarray(col_ok)[None, None], blocks, NEG_BIG)
    blocks = jnp.concatenate([blocks, jnp.full((n_heads, 1, GRID_W, GRID_W), NEG_BIG, F32)], axis=1)
    which = np.full((len(sets), ATT_PAIR, ATT_BAND), n_dr, np.int32)
    for si, (kind, p, off, rs_rel, m0) in enumerate(sets):
        for a in range(ATT_PAIR):
            rq = ATT_PAIR * p + a if kind != "mid" else a
            rs = rs_rel[a] if kind != "mid" else a
            base = m0 if kind != "mid" else 0
            for j in range(ATT_BAND):
                rk = 2 * base + j
                if rs <= rk < rs + WIN_H:
                    which[si, a, j] = rk - rq - off + (WIN_H - 1)
    pairs = which.reshape(len(sets), ATT_PAIR, ATT_BAND // 2, 2)
    uniq, inv = np.unique(pairs.reshape(-1, 2), axis=0, return_inverse=True)
    lane_blocks = jnp.concatenate([jnp.take(blocks, jnp.asarray(uniq[:, 0]), axis=1),
                                   jnp.take(blocks, jnp.asarray(uniq[:, 1]), axis=1)], axis=-1)
    tab = jnp.take(lane_blocks, jnp.asarray(inv.reshape(-1)), axis=1)
    tab = tab.reshape(n_heads // 2, 2, len(sets), ATT_PAIR, ATT_BAND // 2, GRID_W, 2 * GRID_W)
    tab = tab.transpose(0, 2, 4, 1, 3, 5, 6)
    return tab.reshape(n_heads // 2, len(sets), ATT_BAND // 2, 2 * ATT_PAIR * GRID_W, 2 * GRID_W)


def _attn_kernel(q_ref, k0, k1, k2, k3, v0, v1, v2, v3, kc_ref, vc_ref, tb_ref, o_ref,
                 kwin, vwin, vcx, *, m0_top, m0_bot, bounded):
    i = pl.program_id(2)
    nt = pl.num_programs(2)
    ck = k0.shape[2]
    for c, (kr, vr) in enumerate(((k0, v0), (k1, v1), (k2, v2), (k3, v3))):
        kwin[c * ck:(c + 1) * ck, :] = kr[0, 0]
        vwin[c * ck:(c + 1) * ck, 0:LANES] = vr[0, 0]
    kc = kc_ref[0]
    if bounded:
        vwin[:, LANES:] = jnp.ones((vwin.shape[0], LANES), BF16)
        vcx[:, 0:LANES] = vc_ref[0]
        vcx[:, LANES:] = jnp.ones((vcx.shape[0], LANES), BF16)
        vc = vcx[...]
    else:
        vc = vc_ref[0]
    lane = lax.broadcasted_iota(jnp.int32, (1, LANES), 1)
    head_lo = lane < (LANES // 2)
    nq = ATT_PAIR * GRID_W
    nk = ATT_BAND * GRID_W
    npair = ATT_ROWS // ATT_PAIR
    dn = (((1,), (1,)), ((), ()))
    q2s, ss, vbs = [], [], []
    for p in range(npair):
        m0 = jnp.where(i == 0, m0_top[p], jnp.where(i == nt - 1, m0_bot[p], p))
        st = jnp.where(i == 0, 1 + p, jnp.where(i == nt - 1, 1 + npair + p, 0))
        start = pl.multiple_of(m0 * LANES, LANES)
        kb = kwin[pl.ds(start, nk), :]
        vbs.append(vwin[pl.ds(start, nk), :])
        qp = q_ref[0, p * nq:(p + 1) * nq, :]
        zero = jnp.zeros_like(qp)
        q2 = jnp.concatenate([jnp.where(head_lo, qp, zero), jnp.where(head_lo, zero, qp)], axis=0)
        q2s.append(q2)
        bias = jnp.concatenate([tb_ref[0, st, lb] for lb in range(tb_ref.shape[2])], axis=1)
        ss.append(lax.dot_general(q2, kb, dn, preferred_element_type=F32) + bias)
    sc_all = lax.dot_general(jnp.concatenate(q2s, axis=0), kc, dn, preferred_element_type=F32)
    for p in range(npair):
        s = ss[p]
        vb = vbs[p]
        sc = sc_all[p * 2 * nq:(p + 1) * 2 * nq]
        if bounded:
            ov = (jnp.dot(jnp.exp(s).astype(BF16), vb, preferred_element_type=F32)
                  + jnp.dot(jnp.exp(sc).astype(BF16), vc, preferred_element_type=F32))
            o2 = ov[:, 0:LANES] * (1.0 / ov[:, LANES:])
        else:
            m = jnp.maximum(jnp.max(s, axis=-1, keepdims=True), jnp.max(sc, axis=-1, keepdims=True))
            e = jnp.exp(s - m)
            ec = jnp.exp(sc - m)
            den = jnp.sum(e, axis=-1, keepdims=True) + jnp.sum(ec, axis=-1, keepdims=True)
            o2 = (jnp.dot(e.astype(BF16), vb, preferred_element_type=F32)
                  + jnp.dot(ec.astype(BF16), vc, preferred_element_type=F32)) * (1.0 / den)
        o_ref[0, p * nq:(p + 1) * nq, :] = jnp.where(head_lo, o2[:nq], o2[nq:]).astype(BF16)


def _attention(q, k, v, kc, vc, bias_tab, bounded):
    b, s, d = q.shape
    vlanes = 2 * LANES if bounded else LANES
    c = kc.shape[1]
    rows = s // GRID_W
    assert rows % ATT_ROWS == 0 and rows >= ATT_KROWS
    tq = ATT_ROWS * GRID_W
    nt = s // tq
    chunk_rows = 4
    ck = chunk_rows * GRID_W
    nchunk = ATT_KROWS // chunk_rows
    k4 = k.reshape(b, s // ck, ck, d)
    v4 = v.reshape(b, s // ck, ck, d)
    sets = _att_band_layout(None)
    npair = ATT_ROWS // ATT_PAIR
    m0_top = tuple(sets[1 + p][4] for p in range(npair))
    m0_bot = tuple(sets[1 + npair + p][4] for p in range(npair))

    def kmap(t):
        def f(bi, j, i):
            c0 = jnp.clip(i * (ATT_ROWS // chunk_rows) - (WIN_H // 2) // chunk_rows, 0,
                          (rows - ATT_KROWS) // chunk_rows)
            return (bi, c0 + t, 0, j)
        return f

    kv_specs = [pl.BlockSpec((1, 1, ck, LANES), kmap(t)) for t in range(nchunk)]
    return pl.pallas_call(
        functools.partial(_attn_kernel, m0_top=m0_top, m0_bot=m0_bot, bounded=bounded),
        grid=(b, d // LANES, nt),
        in_specs=[pl.BlockSpec((1, tq, LANES), lambda bi, j, i: (bi, i, j))]
                 + kv_specs + kv_specs
                 + [pl.BlockSpec((1, c, LANES), lambda bi, j, i: (bi, 0, j)),
                    pl.BlockSpec((1, c, LANES), lambda bi, j, i: (bi, 0, j)),
                    pl.BlockSpec((1,) + bias_tab.shape[1:], lambda bi, j, i: (j, 0, 0, 0, 0))],
        out_specs=pl.BlockSpec((1, tq, LANES), lambda bi, j, i: (bi, i, j)),
        out_shape=jax.ShapeDtypeStruct((b, s, d), BF16),
        scratch_shapes=[pltpu.VMEM((ATT_KROWS * GRID_W, LANES), BF16),
                        pltpu.VMEM((ATT_KROWS * GRID_W, vlanes), BF16),
                        pltpu.VMEM((c, vlanes), BF16)],
        compiler_params=_cparams(("parallel", "parallel", "arbitrary")),
        name="nbr_attention_bounded" if bounded else "nbr_attention",
    )(q, *([k4] * nchunk), *([v4] * nchunk), kc, vc, bias_tab)


def _ctx_attn_kernel(q_ref, k_ref, v_ref, o_ref):
    lane = lax.broadcasted_iota(jnp.int32, (1, LANES), 1)
    head_lo = lane < (LANES // 2)
    q = q_ref[0]
    k = k_ref[0]
    v = v_ref[0]
    outs = []
    for hh in range(2):
        mask = head_lo if hh == 0 else jnp.logical_not(head_lo)
        qh = jnp.where(mask, q, jnp.zeros_like(q))
        s = lax.dot_general(qh, k, (((1,), (1,)), ((), ())), preferred_element_type=F32)
        m = jnp.max(s, axis=-1, keepdims=True)
        e = jnp.exp(s - m)
        den = jnp.sum(e, axis=-1, keepdims=True)
        o = jnp.dot(e.astype(BF16), v, preferred_element_type=F32)
        outs.append(o * (1.0 / den))
    o_ref[0] = jnp.where(head_lo, outs[0], outs[1]).astype(BF16)


def _ctx_attention(q, k, v):
    b, c, d = q.shape
    spec = pl.BlockSpec((1, c, LANES), lambda bi, j: (bi, 0, j))
    return pl.pallas_call(
        _ctx_attn_kernel,
        grid=(b, d // LANES),
        in_specs=[spec, spec, spec],
        out_specs=spec,
        out_shape=jax.ShapeDtypeStruct((b, c, d), BF16),
        compiler_params=_cparams(("parallel", "parallel")),
        name="ctx_attention",
    )(q, k, v)


def _proj_res_kernel(a_ref, w_ref, x_ref, gate_ref, o_ref):
    y = jnp.dot(a_ref[0], w_ref[...], preferred_element_type=F32)
    o_ref[0] = x_ref[0] + gate_ref[0] * y


def _proj_residual(a, w_bf, x, gate, tm):
    bx, sx, kdim = a.shape
    d = x.shape[-1]
    row = lambda b, i: (b, i, 0)
    return pl.pallas_call(
        _proj_res_kernel,
        grid=(bx, sx // tm),
        in_specs=[pl.BlockSpec((1, tm, kdim), row),
                  pl.BlockSpec((kdim, d), lambda b, i: (0, 0)),
                  pl.BlockSpec((1, tm, d), row),
                  pl.BlockSpec((1, 1, d), lambda b, i: (b, 0, 0))],
        out_specs=pl.BlockSpec((1, tm, d), row),
        out_shape=jax.ShapeDtypeStruct(x.shape, F32),
        compiler_params=_cparams(("parallel", "parallel")),
        name="proj_residual",
    )(a, w_bf, x, gate)


def _swiglu_chunks(h, w1_at, w3_at, w2_at, ff, tf):
    acc = None
    for c0 in range(0, ff, tf):
        a = jnp.dot(h, w1_at(c0, tf), preferred_element_type=F32)
        b = jnp.dot(h, w3_at(c0, tf), preferred_element_type=F32)
        t = (a * jax.nn.sigmoid(a) * b).astype(BF16)
        y = jnp.dot(t, w2_at(c0, tf), preferred_element_type=F32)
        acc = y if acc is None else acc + y
    return acc


def _ffn_kernel(x_ref, g_ref, sh_ref, sc_ref, gate_ref, w1_ref, w3_ref, w2_ref, o_ref, *, tf):
    x = x_ref[0]
    h = _rms_mod(x, g_ref[...], sh_ref[0], sc_ref[0]).astype(BF16)
    y = _swiglu_chunks(h, lambda c0, n: w1_ref[:, c0:c0 + n], lambda c0, n: w3_ref[:, c0:c0 + n],
                       lambda c0, n: w2_ref[c0:c0 + n, :], w1_ref.shape[1], tf)
    o_ref[0] = x + gate_ref[0] * y


def _ffn(x, norm_g, shift, scale, gate, w1_bf, w3_bf, w2_bf, tm, tf):
    bx, sx, d = x.shape
    ff = w1_bf.shape[1]
    row = lambda b, i: (b, i, 0)
    vec = lambda b, i: (b, 0, 0)
    const2 = lambda b, i: (0, 0)
    once = pl.Buffered(1)
    return pl.pallas_call(
        functools.partial(_ffn_kernel, tf=tf),
        grid=(bx, sx // tm),
        in_specs=[pl.BlockSpec((1, tm, d), row),
                  pl.BlockSpec((1, d), const2),
                  pl.BlockSpec((1, 1, d), vec),
                  pl.BlockSpec((1, 1, d), vec),
                  pl.BlockSpec((1, 1, d), vec),
                  pl.BlockSpec((d, ff), const2, pipeline_mode=once),
                  pl.BlockSpec((d, ff), const2, pipeline_mode=once),
                  pl.BlockSpec((ff, d), const2, pipeline_mode=once)],
        out_specs=pl.BlockSpec((1, tm, d), row),
        out_shape=jax.ShapeDtypeStruct(x.shape, F32),
        compiler_params=_cparams(("parallel", "parallel")),
        name="ffn_swiglu",
    )(x, norm_g.reshape(1, d), shift, scale, gate, w1_bf, w3_bf, w2_bf)


def _inproj_kernel(x_ref, xp_ref, xn_ref, g_ref, sh_ref, sc_ref, w_ref, cw_ref, cb_ref, gb_ref, xc_ref, ext,
                   *, dr):
    i = pl.program_id(1)
    nb = pl.num_programs(1)
    tm = x_ref.shape[1]
    pad = SUBLANES
    xe = jnp.concatenate([xp_ref[0], x_ref[0], xn_ref[0]], axis=0)
    h = _rms_mod(xe, g_ref[...], sh_ref[0], sc_ref[0]).astype(BF16)
    xr = jnp.dot(h, w_ref[:, dr:], preferred_element_type=F32)
    rowi = lax.broadcasted_iota(jnp.int32, (tm + 2 * pad, 1), 0)
    inside = jnp.logical_and(jnp.logical_or(rowi >= pad, i > 0),
                             jnp.logical_or(rowi < tm + pad, i < nb - 1))
    ext[...] = jnp.where(inside, xr, 0.0)
    gb_ref[0] = jnp.dot(h, w_ref[:, :dr], preferred_element_type=F32)[pad:pad + tm]
    cw = cw_ref[...]
    xc = cb_ref[...] + sum(
        cw[j:j + 1, :] * ext[pad + j - CONV_LEFT:pad + j - CONV_LEFT + tm, :] for j in range(cw.shape[0]))
    for k in range(dr // LANES):
        xc_ref[0, k] = xc[:, k * LANES:(k + 1) * LANES]


def _inproj(x, norm_g, shift, scale, w_in_bf, conv_w, conv_b, tm):
    bx, sx, d = x.shape
    dr = w_in_bf.shape[1] // 2
    tb = tm // SUBLANES
    row = lambda b, i: (b, i, 0)
    vec = lambda b, i: (b, 0, 0)
    const2 = lambda b, i: (0, 0)
    out = jax.ShapeDtypeStruct((bx, sx, dr), F32)
    return pl.pallas_call(
        functools.partial(_inproj_kernel, dr=dr),
        grid=(bx, sx // tm),
        in_specs=[pl.BlockSpec((1, tm, d), row),
                  pl.BlockSpec((1, SUBLANES, d), lambda b, i: (b, jnp.maximum(i * tb - 1, 0), 0)),
                  pl.BlockSpec((1, SUBLANES, d),
                               lambda b, i: (b, jnp.minimum((i + 1) * tb, sx // SUBLANES - 1), 0)),
                  pl.BlockSpec((1, d), const2),
                  pl.BlockSpec((1, 1, d), vec),
                  pl.BlockSpec((1, 1, d), vec),
                  pl.BlockSpec((d, 2 * dr), const2),
                  pl.BlockSpec(conv_w.shape, const2),
                  pl.BlockSpec((1, dr), const2)],
        out_specs=[pl.BlockSpec((1, tm, dr), row),
                   pl.BlockSpec((1, dr // LANES, tm, LANES), lambda b, i: (b, 0, i, 0))],
        out_shape=[out, jax.ShapeDtypeStruct((bx, dr // LANES, sx, LANES), F32)],
        scratch_shapes=[pltpu.VMEM((tm + 2 * SUBLANES, dr), F32)],
        compiler_params=_cparams(("parallel", "parallel")),
        name="rglru_inproj",
    )(x, x, x, norm_g.reshape(1, d), shift, scale, w_in_bf, conv_w, conv_b.reshape(1, dr))


def _lru_scan_kernel(xc_hbm, wa_ref, wx_ref, ba_ref, bx_ref, lam_ref, h0_ref, o_hbm, xbuf, obuf, carry,
                     sem_in, sem_out, *, reverse, ngroups):
    bi = pl.program_id(0)
    i = pl.program_id(1)
    nb = pl.num_programs(1)
    ncg, nl = xbuf.shape[1], xbuf.shape[2]
    t = nl * SUBLANES
    c = ncg * LANES
    slot = i % 2

    def block_start(step):
        return ((nb - 1 - step) if reverse else step) * t

    def in_copies(step, to_slot):
        return [pltpu.make_async_copy(xc_hbm.at[bi, :, pl.ds(block_start(step) + ch * nl, nl), :],
                                      xbuf.at[to_slot, :, :, ch, :], sem_in.at[to_slot])
                for ch in range(SUBLANES)]

    def out_copies(step, from_slot):
        return [pltpu.make_async_copy(obuf.at[from_slot, :, :, ch, :],
                                      o_hbm.at[bi, :, pl.ds(block_start(step) + ch * nl, nl), :],
                                      sem_out.at[from_slot])
                for ch in range(SUBLANES)]

    @pl.when(i == 0)
    def _():
        carry[...] = jnp.broadcast_to(h0_ref[0], carry.shape)
        for cp in in_copies(0, 0):
            cp.start()

    @pl.when(i + 1 < nb)
    def _():
        for cp in in_copies(i + 1, 1 - slot):
            cp.start()

    for cp in in_copies(i, slot):
        cp.wait()

    @pl.when(i >= 2)
    def _():
        for cp in out_copies(i, slot):
            cp.wait()

    gw = c // ngroups
    kpg = gw // LANES
    steps = range(nl - 1, -1, -1) if reverse else range(nl)
    rows = lambda v, j: v[j * SUBLANES:(j + 1) * SUBLANES]
    sub = lax.broadcasted_iota(jnp.int32, (SUBLANES, 1), 0)
    for gi in range(ngroups):
        cs = slice(gi * gw, (gi + 1) * gw)
        xc = jnp.concatenate([xbuf[slot, gi * kpg + k].reshape(t, LANES) for k in range(kpg)], axis=1)
        xb = xc.astype(BF16)
        t_a = jnp.tanh(jnp.dot(xb, wa_ref[gi], preferred_element_type=F32) + 0.5 * ba_ref[:, cs])
        t_x = jnp.tanh(jnp.dot(xb, wx_ref[gi], preferred_element_type=F32) + 0.5 * bx_ref[:, cs])
        lam = lam_ref[:, cs]
        sp = jnp.maximum(-lam, 0.0) + jnp.log(1.0 + jnp.exp(-jnp.abs(lam)))
        k1 = ((-0.5 * LRU_C) * np.float32(np.log2(np.e))) * sp
        a = jnp.exp2(t_a * k1 + k1)
        om = 1.0 - a * a
        u = (om * lax.rsqrt(jnp.maximum(om, 1e-37))) * ((0.5 * t_x + 0.5) * xc)

        hloc = jnp.zeros((SUBLANES, gw), F32)
        aprod = jnp.ones((SUBLANES, gw), F32)
        for j in steps:
            hloc = rows(a, j) * hloc + rows(u, j)
            aprod = rows(a, j) * aprod

        hprev = carry[:, cs]
        for sft in (1, 2, 4):
            if reverse:
                a_sh = pltpu.roll(aprod, SUBLANES - sft, axis=0)
                h_sh = pltpu.roll(hloc, SUBLANES - sft, axis=0)
                ok = sub < SUBLANES - sft
            else:
                a_sh = pltpu.roll(aprod, sft, axis=0)
                h_sh = pltpu.roll(hloc, sft, axis=0)
                ok = sub >= sft
            hloc = jnp.where(ok, aprod * h_sh + hloc, hloc)
            aprod = jnp.where(ok, aprod * a_sh, aprod)
        after = hloc + aprod * hprev
        if reverse:
            h = jnp.where(sub == SUBLANES - 1, hprev, pltpu.roll(after, SUBLANES - 1, axis=0))
            carry[:, cs] = jnp.broadcast_to(after[0:1], after.shape)
        else:
            h = jnp.where(sub == 0, hprev, pltpu.roll(after, 1, axis=0))
            carry[:, cs] = jnp.broadcast_to(after[SUBLANES - 1:SUBLANES], after.shape)

        for j in steps:
            h = rows(a, j) * h + rows(u, j)
            for k in range(kpg):
                obuf[slot, gi * kpg + k, j] = h[:, k * LANES:(k + 1) * LANES]
    for cp in out_copies(i, slot):
        cp.start()

    @pl.when(i == nb - 1)
    def _():
        for cp in out_copies(i, slot):
            cp.wait()

    @pl.when(jnp.logical_and(i == nb - 1, nb >= 2))
    def _():
        for cp in out_copies(i, 1 - slot):
            cp.wait()


def _lru_scan(xc, wa_g, wx_g, ba, bx, lam, h0, reverse, t):
    b, ncg, s, _ = xc.shape
    c = ncg * LANES
    nb = s // t
    nl = t // SUBLANES
    ngroups = wa_g.shape[0]
    gw = c // ngroups
    const2 = lambda bi, i: (0, 0)
    const3 = lambda bi, i: (0, 0, 0)
    return pl.pallas_call(
        functools.partial(_lru_scan_kernel, reverse=reverse, ngroups=ngroups),
        grid=(b, nb),
        in_specs=[pl.BlockSpec(memory_space=pl.ANY),
                  pl.BlockSpec((ngroups, gw, gw), const3),
                  pl.BlockSpec((ngroups, gw, gw), const3),
                  pl.BlockSpec((1, c), const2),
                  pl.BlockSpec((1, c), const2),
                  pl.BlockSpec((1, c), const2),
                  pl.BlockSpec((1, 1, c), lambda bi, i: (bi, 0, 0))],
        out_specs=pl.BlockSpec(memory_space=pl.ANY),
        out_shape=jax.ShapeDtypeStruct((b, ncg, s, LANES), F32),
        scratch_shapes=[pltpu.VMEM((2, ncg, nl, SUBLANES, LANES), F32),
                        pltpu.VMEM((2, ncg, nl, SUBLANES, LANES), F32),
                        pltpu.VMEM((SUBLANES, c), F32),
                        pltpu.SemaphoreType.DMA((2,)),
                        pltpu.SemaphoreType.DMA((2,))],
        compiler_params=_cparams(("parallel", "arbitrary")),
        name="lru_scan_bwd" if reverse else "lru_scan_fwd",
    )(xc, wa_g, wx_g, ba.reshape(1, c), bx.reshape(1, c), lam.reshape(1, c), h0)


def _lru_out_kernel(g_ref, hf_ref, hb_ref, w_ref, x_ref, gate_ref, o_ref):
    hs = jnp.concatenate([hf_ref[0, k] + hb_ref[0, k] for k in range(hf_ref.shape[1])], axis=1)
    a = (jax.nn.gelu(g_ref[0]) * hs).astype(BF16)
    y = jnp.dot(a, w_ref[...], preferred_element_type=F32)
    o_ref[0] = x_ref[0] + gate_ref[0] * y


def _lru_out(g, hf, hb, w_out_bf, x, gate, tm):
    bx, sx, dr = g.shape
    d = x.shape[-1]
    row = lambda b, i: (b, i, 0)
    cg = pl.BlockSpec((1, dr // LANES, tm, LANES), lambda b, i: (b, 0, i, 0))
    return pl.pallas_call(
        _lru_out_kernel,
        grid=(bx, sx // tm),
        in_specs=[pl.BlockSpec((1, tm, dr), row), cg, cg]
                 + [pl.BlockSpec((dr, d), lambda b, i: (0, 0)),
                    pl.BlockSpec((1, tm, d), row),
                    pl.BlockSpec((1, 1, d), lambda b, i: (b, 0, 0))],
        out_specs=pl.BlockSpec((1, tm, d), row),
        out_shape=jax.ShapeDtypeStruct(x.shape, F32),
        compiler_params=_cparams(("parallel", "parallel")),
        name="rglru_outproj",
    )(g, hf, hb, w_out_bf, x, gate)


def _router_kernel(x_ref, g_ref, sh_ref, sc_ref, wh_ref, wl_ref, br_ref, hp_ref, rt_ref, *, n_exp):
    h = _rms_mod(x_ref[0], g_ref[...], sh_ref[0], sc_ref[0])
    hb = h.astype(BF16)
    hf = hb.astype(F32)
    h_lo = (h - hf).astype(BF16)
    logits = (jnp.dot(hb, wh_ref[...], preferred_element_type=F32)
              + jnp.dot(hb, wl_ref[...], preferred_element_type=F32)
              + jnp.dot(h_lo, wh_ref[...], preferred_element_type=F32)) + br_ref[...]
    lane = lax.broadcasted_iota(jnp.int32, logits.shape, 1)
    logits = jnp.where(lane < n_exp, logits, -jnp.inf)
    m1 = jnp.max(logits, axis=-1, keepdims=True)
    i1 = jnp.min(jnp.where(logits == m1, lane, LANES), axis=-1, keepdims=True)
    rest = jnp.where(lane == i1, -jnp.inf, logits)
    m2 = jnp.max(rest, axis=-1, keepdims=True)
    i2 = jnp.min(jnp.where(rest == m2, lane, LANES), axis=-1, keepdims=True)
    e2 = jnp.exp(m2 - m1)
    g1 = 1.0 / (1.0 + e2)
    g2 = e2 * g1
    rt_ref[0] = jnp.where(lane == 0, g1, jnp.where(lane == 1, g2, jnp.where(
        lane == 2, i1.astype(F32), jnp.where(lane == 3, i2.astype(F32), 0.0))))
    hp_ref[0] = h


def _router(x, norm_g, shift, scale, w_router, b_router, tm):
    bx, sx, d = x.shape
    n_exp = w_router.shape[1]
    wpad = jnp.zeros((d, LANES), F32).at[:, :n_exp].set(w_router)
    w_hi = wpad.astype(BF16)
    w_lo = (wpad - w_hi.astype(F32)).astype(BF16)
    bpad = jnp.zeros((1, LANES), F32).at[0, :n_exp].set(b_router)
    row = lambda b, i: (b, i, 0)
    vec = lambda b, i: (b, 0, 0)
    const2 = lambda b, i: (0, 0)
    return pl.pallas_call(
        functools.partial(_router_kernel, n_exp=n_exp),
        grid=(bx, sx // tm),
        in_specs=[pl.BlockSpec((1, tm, d), row),
                  pl.BlockSpec((1, d), const2),
                  pl.BlockSpec((1, 1, d), vec),
                  pl.BlockSpec((1, 1, d), vec),
                  pl.BlockSpec((d, LANES), const2),
                  pl.BlockSpec((d, LANES), const2),
                  pl.BlockSpec((1, LANES), const2)],
        out_specs=[pl.BlockSpec((1, tm, d), row), pl.BlockSpec((1, tm, LANES), row)],
        out_shape=[jax.ShapeDtypeStruct((bx, sx, d), F32),
                   jax.ShapeDtypeStruct((bx, sx, LANES), F32)],
        compiler_params=_cparams(("parallel", "parallel")),
        name="moe_router",
    )(x, norm_g.reshape(1, d), shift, scale, w_hi, w_lo, bpad)


def _dispatch_kernel(dest_ref, fill_ref, hp_ref, xs_ref, zrow, sem, zsem, *, td, n_fill):
    base = pl.program_id(0) * td

    @pl.when(pl.program_id(0) == 0)
    def _():
        zrow[...] = jnp.zeros_like(zrow)

        def zissue(r, _):
            pltpu.make_async_copy(zrow.at[pl.ds(0, 1)], xs_ref.at[pl.ds(fill_ref[r], 1)], zsem).start()
            return 0

        lax.fori_loop(0, n_fill, zissue, 0, unroll=8)

    def issue(r, _):
        for kk in range(TOP_K):
            dst = dest_ref[(base + r) * TOP_K + kk]
            pltpu.make_async_copy(hp_ref.at[pl.ds(r, 1)], xs_ref.at[pl.ds(dst, 1)], sem).start()
        return 0

    lax.fori_loop(0, td, issue, 0, unroll=8)
    for kk in range(TOP_K):
        pltpu.make_async_copy(hp_ref, xs_ref.at[pl.ds(0, td)], sem).wait()

    @pl.when(pl.program_id(0) == 0)
    def _():
        pltpu.make_async_copy(xs_ref.at[pl.ds(0, n_fill)], xs_ref.at[pl.ds(0, n_fill)], zsem).wait()


def _dispatch(hp, dest, fill, n_rows, td):
    n, w = hp.shape
    n_fill = fill.shape[0]
    return pl.pallas_call(
        functools.partial(_dispatch_kernel, td=td, n_fill=n_fill),
        grid_spec=pltpu.PrefetchScalarGridSpec(
            num_scalar_prefetch=2,
            grid=(n // td,),
            in_specs=[pl.BlockSpec((td, w), lambda i, dref, fref: (i, 0))],
            out_specs=pl.BlockSpec(memory_space=pl.ANY),
            scratch_shapes=[pltpu.VMEM((SUBLANES, w), hp.dtype),
                            pltpu.SemaphoreType.DMA(()), pltpu.SemaphoreType.DMA(())]),
        out_shape=jax.ShapeDtypeStruct((n_rows, w), hp.dtype),
        compiler_params=pltpu.CompilerParams(dimension_semantics=("arbitrary",),
                                             vmem_limit_bytes=VMEM_LIMIT, has_side_effects=True),
        name="moe_dispatch",
    )(dest, fill, hp)


def _experts_kernel(te_ref, nu_ref, x_ref, w1_ref, w3_ref, w2_ref, o_ref, *, tf):
    del te_ref
    i = pl.program_id(0)

    @pl.when(i < nu_ref[0])
    def _():
        o_ref[...] = _swiglu_chunks(
            x_ref[...].astype(BF16), lambda c0, n: w1_ref[0, :, c0:c0 + n],
            lambda c0, n: w3_ref[0, :, c0:c0 + n], lambda c0, n: w2_ref[0, c0:c0 + n, :],
            w1_ref.shape[2], tf)

    @pl.when(i >= nu_ref[0])
    def _():
        o_ref[...] = jnp.zeros_like(o_ref)


def _experts(xs, tile_expert, n_used, w1_bf, w3_bf, w2_bf, tm, tf):
    n_rows, w = xs.shape
    n_exp, d, ff = w1_bf.shape
    n_tiles = n_rows // tm

    def live(i, nu):
        return jnp.minimum(i, nu[0] - 1)

    once = pl.Buffered(1)
    return pl.pallas_call(
        functools.partial(_experts_kernel, tf=tf),
        grid_spec=pltpu.PrefetchScalarGridSpec(
            num_scalar_prefetch=2,
            grid=(n_tiles,),
            in_specs=[pl.BlockSpec((tm, w), lambda i, te, nu: (live(i, nu), 0)),
                      pl.BlockSpec((1, d, ff), lambda i, te, nu: (te[live(i, nu)], 0, 0), pipeline_mode=once),
                      pl.BlockSpec((1, d, ff), lambda i, te, nu: (te[live(i, nu)], 0, 0), pipeline_mode=once),
                      pl.BlockSpec((1, ff, d), lambda i, te, nu: (te[live(i, nu)], 0, 0), pipeline_mode=once)],
            out_specs=pl.BlockSpec((tm, d), lambda i, te, nu: (i, 0))),
        out_shape=jax.ShapeDtypeStruct((n_rows, d), F32),
        compiler_params=_cparams(("arbitrary",)),
        name="moe_experts",
    )(tile_expert, n_used, xs, w1_bf, w3_bf, w2_bf)


def _combine_kernel(pos_ref, x_ref, gate_ref, rt_ref, y_ref, o_ref, ybuf, sem, *, tc):
    i = pl.program_id(0)
    slot = i % 2

    def issue(tile, to_slot):
        base = tile * tc

        def body(r, _):
            for kk in range(TOP_K):
                src = pos_ref[(base + r) * TOP_K + kk]
                pltpu.make_async_copy(y_ref.at[pl.ds(src, 1)], ybuf.at[to_slot, kk, pl.ds(r, 1)],
                                      sem.at[to_slot]).start()
            return 0

        lax.fori_loop(0, tc, body, 0, unroll=8)

    @pl.when(i == 0)
    def _():
        issue(0, 0)

    @pl.when(i + 1 < pl.num_programs(0))
    def _():
        issue(i + 1, 1 - slot)

    for kk in range(TOP_K):
        pltpu.make_async_copy(y_ref.at[pl.ds(0, tc)], ybuf.at[slot, kk], sem.at[slot]).wait()
    rt = rt_ref[...]
    y = rt[:, 0:1] * ybuf[slot, 0] + rt[:, 1:2] * ybuf[slot, 1]
    o_ref[...] = x_ref[...] + gate_ref[0] * y


def _combine(x2, gate, route, y, pos, seq, tc):
    n, d = x2.shape
    return pl.pallas_call(
        functools.partial(_combine_kernel, tc=tc),
        grid_spec=pltpu.PrefetchScalarGridSpec(
            num_scalar_prefetch=1,
            grid=(n // tc,),
            in_specs=[pl.BlockSpec((tc, d), lambda i, p: (i, 0)),
                      pl.BlockSpec((1, 1, d), lambda i, p: ((i * tc) // seq, 0, 0)),
                      pl.BlockSpec((tc, LANES), lambda i, p: (i, 0)),
                      pl.BlockSpec(memory_space=pl.ANY)],
            out_specs=pl.BlockSpec((tc, d), lambda i, p: (i, 0)),
            scratch_shapes=[pltpu.VMEM((2, TOP_K, tc, d), F32), pltpu.SemaphoreType.DMA((2,))]),
        out_shape=jax.ShapeDtypeStruct((n, d), F32),
        compiler_params=_cparams(("arbitrary",)),
        name="moe_combine",
    )(pos, x2, gate, route, y)


def _route_positions(e_flat, n_exp, tm):
    a = e_flat.shape[0]
    onehot = (e_flat[:, None] == jnp.arange(n_exp, dtype=jnp.int32)[None, :]).astype(jnp.int32)
    csum = jnp.cumsum(onehot, axis=0)
    counts = csum[-1]
    padded = ((counts + tm - 1) // tm) * tm
    ends = jnp.cumsum(padded)
    starts = ends - padded
    dest = jnp.sum(onehot * (csum - 1 + starts[None, :]), axis=1).astype(jnp.int32)
    n_tiles = a // tm + n_exp
    tile_ids = jnp.arange(n_tiles, dtype=jnp.int32)
    tile_expert = jnp.minimum(jnp.sum((ends[None, :] // tm <= tile_ids[:, None]).astype(jnp.int32), axis=1),
                              n_exp - 1)
    n_used = (ends[-1] // tm).astype(jnp.int32).reshape(1)
    j = jnp.arange(tm, dtype=jnp.int32)[None, :]
    is_pad = counts[:, None] + j < padded[:, None]
    tail_rank = jnp.cumsum((~is_pad).astype(jnp.int32).reshape(-1)) - 1
    fill = jnp.where(is_pad.reshape(-1), (starts[:, None] + counts[:, None] + j).reshape(-1),
                     ends[-1] + tail_rank).astype(jnp.int32)
    return dest, fill, tile_expert.astype(jnp.int32), n_used, n_tiles * tm


def _blockdiag_groups(w, ngroups):
    nblk, bw, _ = w.shape
    per = nblk // ngroups
    eye = jnp.eye(per, dtype=w.dtype)
    wg = w.reshape(ngroups, per, bw, bw)
    full = wg[:, :, :, None, :] * eye[None, :, None, :, None]
    return full.reshape(ngroups, per * bw, per * bw)


def kernel(x, c, ctx, c_ctx, l0_w_mod, l0_b_mod, l0_norm1, l0_norm2, l0_w_qkv, l0_q_gain, l0_k_gain, l0_rpb, l0_w_o, l0_ffn_w1, l0_ffn_w3, l0_ffn_w2, l1_w_mod, l1_b_mod, l1_norm1, l1_norm2, l1_w_in, l1_conv_w, l1_conv_b, l1_gate_a_w, l1_gate_a_b, l1_gate_x_w, l1_gate_x_b, l1_lam, l1_w_out, l1_router_w, l1_router_b, l1_moe_w1, l1_moe_w3, l1_moe_w2):
    b, s, d = x.shape
    n_ctx = ctx.shape[1]
    head_dim = l0_q_gain.shape[0]
    n_heads = d // head_dim
    n_exp = l1_router_w.shape[1]
    tm = min(512, s)
    tmc = min(512, n_ctx)

    def mods(w_mod, b_mod):
        m = _adaln_mod(jnp.concatenate([c, c_ctx[None, :]], axis=0), w_mod, b_mod)
        lat = [m[:b, j * d:(j + 1) * d].reshape(b, 1, d) for j in range(6)]
        cx = [jnp.broadcast_to(m[b:b + 1, j * d:(j + 1) * d].reshape(1, 1, d), (b, 1, d)) for j in range(6)]
        return lat, cx

    ml, mc = mods(l0_w_mod, l0_b_mod)
    w_qkv = l0_w_qkv.astype(BF16)
    qg = (jnp.tile(l0_q_gain, n_heads) * (head_dim ** -0.5)).reshape(1, d)
    kg = jnp.tile(l0_k_gain, n_heads).reshape(1, d)
    ql, kl, vl = _qkv(x, l0_norm1, ml[0], ml[1], w_qkv, qg, kg, head_dim, tm)
    qc, kc, vc = _qkv(ctx, l0_norm1, mc[0], mc[1], w_qkv, qg, kg, head_dim, tmc)
    logit_bound = (1.02 * head_dim) * jnp.max(jnp.abs(qg)) * jnp.max(jnp.abs(kg)) + jnp.max(jnp.abs(l0_rpb))
    bias_tab = _att_bias_table(l0_rpb)
    o_l = lax.cond(logit_bound <= ATT_LOGIT_BOUND,
                   lambda: _attention(ql, kl, vl, kc, vc, bias_tab, True),
                   lambda: _attention(ql, kl, vl, kc, vc, bias_tab, False))
    o_c = _ctx_attention(qc, kc, vc)
    w_o = l0_w_o.astype(BF16)
    x_l = _proj_residual(o_l, w_o, x, ml[2], tm)
    x_c = _proj_residual(o_c, w_o, ctx, mc[2], tmc)
    w1, w3, w2 = l0_ffn_w1.astype(BF16), l0_ffn_w3.astype(BF16), l0_ffn_w2.astype(BF16)
    tf0 = l0_ffn_w1.shape[1]
    x_l = _ffn(x_l, l0_norm2, ml[3], ml[4], ml[5], w1, w3, w2, tm, tf0)
    x_c = _ffn(x_c, l0_norm2, mc[3], mc[4], mc[5], w1, w3, w2, tmc, tf0)

    ml, mc = mods(l1_w_mod, l1_b_mod)
    w_in = l1_w_in.astype(BF16)
    g_l, xr_l = _inproj(x_l, l1_norm1, ml[0], ml[1], w_in, l1_conv_w, l1_conv_b, tm)
    _, xr_c = _inproj(x_c, l1_norm1, mc[0], mc[1], w_in, l1_conv_w, l1_conv_b, tmc)
    dr = g_l.shape[-1]
    ngroups = 4
    t_scan = min(256, n_ctx)
    zeros0 = jnp.zeros((b, 1, dr), F32)
    hs = []
    for di, rev in enumerate((False, True)):
        wa_g = (0.5 * _blockdiag_groups(l1_gate_a_w[di], ngroups)).astype(BF16)
        wx_g = (0.5 * _blockdiag_groups(l1_gate_x_w[di], ngroups)).astype(BF16)
        args = (wa_g, wx_g, l1_gate_a_b[di], l1_gate_x_b[di], l1_lam[di])
        h_c = _lru_scan(xr_c, *args, zeros0, rev, t_scan)
        h0 = (h_c[:, :, 0, :] if rev else h_c[:, :, n_ctx - 1, :]).reshape(b, 1, dr)
        hs.append(_lru_scan(xr_l, *args, h0, rev, t_scan))
    x_l = _lru_out(g_l, hs[0], hs[1], l1_w_out.astype(BF16), x_l, ml[2], tm)

    hp, route = _router(x_l, l1_norm2, ml[3], ml[4], l1_router_w, l1_router_b, tm)
    n = b * s
    route2 = route.reshape(n, LANES)
    e_flat = route2[:, 2:2 + TOP_K].astype(jnp.int32).reshape(n * TOP_K)
    tme = 512
    dest, fill, tile_expert, n_used, n_rows = _route_positions(e_flat, n_exp, tme)
    xs = _dispatch(hp.reshape(n, d), dest, fill, n_rows, min(512, n))
    y = _experts(xs, tile_expert, n_used, l1_moe_w1.astype(BF16), l1_moe_w3.astype(BF16),
                 l1_moe_w2.astype(BF16), tme, l1_moe_w1.shape[2] // 2)
    out = _combine(x_l.reshape(n, d), ml[5], route2, y, dest, s, min(256, n))
    return out.reshape(b, s, d)
```

```python
import functools

import numpy as np
import jax
import jax.numpy as jnp
from jax import lax
from jax.experimental import pallas as pl
from jax.experimental.pallas import tpu as pltpu

GRID_W = 64
WIN_H = 8
WIN_W = 16
TOP_K = 2
LRU_C = 8.0
EPS = 1e-6
CONV_LEFT = 2
NEG_BIG = -1e30
LANES = 128
SUBLANES = 8
VMEM_LIMIT = 56 * 1024 * 1024

ATT_ROWS = 8
ATT_KROWS = 16
ATT_PAIR = 2
ATT_BAND = 10
ATT_LOGIT_BOUND = 40.0

F32 = jnp.float32
BF16 = jnp.bfloat16


def _cparams(sem):
    return pltpu.CompilerParams(dimension_semantics=sem, vmem_limit_bytes=VMEM_LIMIT)


def _rms_mod(x, g, shift, scale):
    ms = jnp.mean(x * x, axis=-1, keepdims=True)
    y = x * lax.rsqrt(ms + EPS) * g
    return y * (1.0 + scale) + shift


def _mod_kernel(ct_ref, w_ref, b_ref, o_ref, *, nrows):
    ct = ct_ref[...]
    s = ct * jax.nn.sigmoid(ct)
    w = w_ref[...]
    rows = [jnp.sum(w * s[:, m:m + 1], axis=0, keepdims=True) + b_ref[...] for m in range(nrows)]
    rows += [jnp.zeros_like(rows[0])] * (SUBLANES - nrows)
    o_ref[...] = jnp.concatenate(rows, axis=0)


def _adaln_mod(cvecs, w_mod, b_mod):
    nrows, d = cvecs.shape
    n = w_mod.shape[1]
    tn = 768
    ct = jnp.zeros((d, SUBLANES), F32).at[:, :nrows].set(cvecs.T)
    return pl.pallas_call(
        functools.partial(_mod_kernel, nrows=nrows),
        grid=(n // tn,),
        in_specs=[pl.BlockSpec((d, SUBLANES), lambda j: (0, 0)),
                  pl.BlockSpec((d, tn), lambda j: (0, j)),
                  pl.BlockSpec((1, tn), lambda j: (0, j))],
        out_specs=pl.BlockSpec((SUBLANES, tn), lambda j: (0, j)),
        out_shape=jax.ShapeDtypeStruct((SUBLANES, n), F32),
        compiler_params=_cparams(("arbitrary",)),
        name="adaln_mod",
    )(ct, w_mod, b_mod.reshape(1, n))


def _qkv_kernel(x_ref, g_ref, sh_ref, sc_ref, w_ref, qg_ref, kg_ref, pm_ref, pe_ref,
                q_ref, k_ref, v_ref, *, d):
    h = _rms_mod(x_ref[0], g_ref[...], sh_ref[0], sc_ref[0]).astype(BF16)
    for part, (gain_ref, out_ref) in enumerate(((qg_ref, q_ref), (kg_ref, k_ref))):
        y = jnp.dot(h, w_ref[:, part * d:(part + 1) * d], preferred_element_type=F32)
        ms = jnp.dot((y * y).astype(BF16), pm_ref[...], preferred_element_type=F32)
        inv = lax.rsqrt(ms + EPS)
        inv_hi = inv.astype(BF16)
        inv_lo = (inv - inv_hi.astype(F32)).astype(BF16)
        full = (jnp.dot(inv_hi, pe_ref[...], preferred_element_type=F32)
                + jnp.dot(inv_lo, pe_ref[...], preferred_element_type=F32))
        out_ref[0] = (y * full * gain_ref[...]).astype(BF16)
    v_ref[0] = jnp.dot(h, w_ref[:, 2 * d:3 * d], preferred_element_type=F32).astype(BF16)


def _qkv(x, norm_g, shift, scale, w_qkv_bf, q_gain_full, k_gain_full, head_dim, tm):
    bx, sx, d = x.shape
    n_heads = d // head_dim
    head_of = np.arange(d) // head_dim
    pm = np.zeros((d, LANES), np.float32)
    pm[np.arange(d), head_of] = 1.0 / head_dim
    pe = np.zeros((LANES, d), np.float32)
    pe[head_of, np.arange(d)] = 1.0
    assert n_heads <= LANES
    row = lambda b, i: (b, i, 0)
    vec = lambda b, i: (b, 0, 0)
    const2 = lambda b, i: (0, 0)
    out = jax.ShapeDtypeStruct((bx, sx, d), BF16)
    return pl.pallas_call(
        functools.partial(_qkv_kernel, d=d),
        grid=(bx, sx // tm),
        in_specs=[pl.BlockSpec((1, tm, d), row),
                  pl.BlockSpec((1, d), const2),
                  pl.BlockSpec((1, 1, d), vec),
                  pl.BlockSpec((1, 1, d), vec),
                  pl.BlockSpec((d, 3 * d), const2),
                  pl.BlockSpec((1, d), const2),
                  pl.BlockSpec((1, d), const2),
                  pl.BlockSpec((d, LANES), const2),
                  pl.BlockSpec((LANES, d), const2)],
        out_specs=[pl.BlockSpec((1, tm, d), row)] * 3,
        out_shape=[out, out, out],
        compiler_params=_cparams(("parallel", "parallel")),
        name="qkv_headnorm",
    )(x, norm_g.reshape(1, d), shift, scale, w_qkv_bf, q_gain_full, k_gain_full,
      jnp.asarray(pm, BF16), jnp.asarray(pe, BF16))


def _att_band_layout(rows):
    npair = ATT_ROWS // ATT_PAIR
    half_win = WIN_H // 2
    sets = []
    for kind, off in (("mid", half_win), ("top", 0), ("bot", ATT_KROWS - ATT_ROWS)):
        for p in range(npair):
            if kind == "mid" and p > 0:
                continue
            rs_rel = []
            for a in range(ATT_PAIR):
                rq = ATT_PAIR * p + a
                if kind == "mid":
                    rs_rel.append(rq)
                elif kind == "top":
                    rs_rel.append(max(rq - half_win, 0))
                else:
                    rs_rel.append(min(rq + off - half_win, ATT_KROWS - WIN_H))
            m0 = min(rs_rel[0] // 2, (ATT_KROWS - ATT_BAND) // 2)
            sets.append((kind, p, off, rs_rel, m0))
    return sets


def _att_bias_table(rpb):
    n_heads, n_dr, n_dc = rpb.shape
    sets = _att_band_layout(None)
    cq = np.arange(GRID_W)
    cs = np.clip(cq - WIN_W // 2, 0, GRID_W - WIN_W)
    ck = np.arange(GRID_W)
    col_ok = (ck[None, :] >= cs[:, None]) & (ck[None, :] < cs[:, None] + WIN_W)
    dc = ck[None, :] - cq[:, None] + (WIN_W - 1)
    sel = ((dc[:, None, :] == np.arange(n_dc)[None, :, None]) & col_ok[:, None, :]).astype(np.float32)
    blocks = jnp.einsum("hrx,qxk->hrqk", rpb.astype(F32), jnp.asarray(sel), precision=lax.Precision.HIGHEST)
    blocks = jnp.where(jnp.asarray(col_ok)[None, None], blocks, NEG_BIG)
    blocks = jnp.concatenate([blocks, jnp.full((n_heads, 1, GRID_W, GRID_W), NEG_BIG, F32)], axis=1)
    which = np.full((len(sets), ATT_PAIR, ATT_BAND), n_dr, np.int32)
    for si, (kind, p, off, rs_rel, m0) in enumerate(sets):
        for a in range(ATT_PAIR):
            rq = ATT_PAIR * p + a if kind != "mid" else a
            rs = rs_rel[a] if kind != "mid" else a
            base = m0 if kind != "mid" else 0
            for j in range(ATT_BAND):
                rk = 2 * base + j
                if rs <= rk < rs + WIN_H:
                    which[si, a, j] = rk - rq - off + (WIN_H - 1)
    pairs = which.reshape(len(sets), ATT_PAIR, ATT_BAND // 2, 2)
    uniq, inv = np.unique(pairs.reshape(-1, 2), axis=0, return_inverse=True)
    lane_blocks = jnp.concatenate([jnp.take(blocks, jnp.asarray(uniq[:, 0]), axis=1),
                                   jnp.take(blocks, jnp.asarray(uniq[:, 1]), axis=1)], axis=-1)
    tab = jnp.take(lane_blocks, jnp.asarray(inv.reshape(-1)), axis=1)
    tab = tab.reshape(n_heads // 2, 2, len(sets), ATT_PAIR, ATT_BAND // 2, GRID_W, 2 * GRID_W)
    tab = tab.transpose(0, 2, 4, 1, 3, 5, 6)
    return tab.reshape(n_heads // 2, len(sets), ATT_BAND // 2, 2 * ATT_PAIR * GRID_W, 2 * GRID_W)


def _attn_kernel(q_ref, k0, k1, k2, k3, v0, v1, v2, v3, kc_ref, vc_ref, tb_ref, o_ref,
                 kwin, vwin, vcx, *, m0_top, m0_bot, bounded):
    i = pl.program_id(2)
    nt = pl.num_programs(2)
    ck = k0.shape[2]
    for c, (kr, vr) in enumerate(((k0, v0), (k1, v1), (k2, v2), (k3, v3))):
        kwin[c * ck:(c + 1) * ck, :] = kr[0, 0]
        vwin[c * ck:(c + 1) * ck, 0:LANES] = vr[0, 0]
    kc = kc_ref[0]
    if bounded:
        vwin[:, LANES:] = jnp.ones((vwin.shape[0], LANES), BF16)
        vcx[:, 0:LANES] = vc_ref[0]
        vcx[:, LANES:] = jnp.ones((vcx.shape[0], LANES), BF16)
        vc = vcx[...]
    else:
        vc = vc_ref[0]
    lane = lax.broadcasted_iota(jnp.int32, (1, LANES), 1)
    head_lo = lane < (LANES // 2)
    nq = ATT_PAIR * GRID_W
    nk = ATT_BAND * GRID_W
    npair = ATT_ROWS // ATT_PAIR
    dn = (((1,), (1,)), ((), ()))
    q2s, ss, vbs = [], [], []
    for p in range(npair):
        m0 = jnp.where(i == 0, m0_top[p], jnp.where(i == nt - 1, m0_bot[p], p))
        st = jnp.where(i == 0, 1 + p, jnp.where(i == nt - 1, 1 + npair + p, 0))
        start = pl.multiple_of(m0 * LANES, LANES)
        kb = kwin[pl.ds(start, nk), :]
        vbs.append(vwin[pl.ds(start, nk), :])
        qp = q_ref[0, p * nq:(p + 1) * nq, :]
        zero = jnp.zeros_like(qp)
        q2 = jnp.concatenate([jnp.where(head_lo, qp, zero), jnp.where(head_lo, zero, qp)], axis=0)
        q2s.append(q2)
        bias = jnp.concatenate([tb_ref[0, st, lb] for lb in range(tb_ref.shape[2])], axis=1)
        ss.append(lax.dot_general(q2, kb, dn, preferred_element_type=F32) + bias)
    sc_all = lax.dot_general(jnp.concatenate(q2s, axis=0), kc, dn, preferred_element_type=F32)
    for p in range(npair):
        s = ss[p]
        vb = vbs[p]
        sc = sc_all[p * 2 * nq:(p + 1) * 2 * nq]
        if bounded:
            ov = (jnp.dot(jnp.exp(s).astype(BF16), vb, preferred_element_type=F32)
                  + jnp.dot(jnp.exp(sc).astype(BF16), vc, preferred_element_type=F32))
            o2 = ov[:, 0:LANES] * (1.0 / ov[:, LANES:])
        else:
            m = jnp.maximum(jnp.max(s, axis=-1, keepdims=True), jnp.max(sc, axis=-1, keepdims=True))
            e = jnp.exp(s - m)
            ec = jnp.exp(sc - m)
            den = jnp.sum(e, axis=-1, keepdims=True) + jnp.sum(ec, axis=-1, keepdims=True)
            o2 = (jnp.dot(e.astype(BF16), vb, preferred_element_type=F32)
                  + jnp.dot(ec.astype(BF16), vc, preferred_element_type=F32)) * (1.0 / den)
        o_ref[0, p * nq:(p + 1) * nq, :] = jnp.where(head_lo, o2[:nq], o2[nq:]).astype(BF16)


def _attention(q, k, v, kc, vc, bias_tab, bounded):
    b, s, d = q.shape
    vlanes = 2 * LANES if bounded else LANES
    c = kc.shape[1]
    rows = s // GRID_W
    assert rows % ATT_ROWS == 0 and rows >= ATT_KROWS
    tq = ATT_ROWS * GRID_W
    nt = s // tq
    chunk_rows = 4
    ck = chunk_rows * GRID_W
    nchunk = ATT_KROWS // chunk_rows
    k4 = k.reshape(b, s // ck, ck, d)
    v4 = v.reshape(b, s // ck, ck, d)
    sets = _att_band_layout(None)
    npair = ATT_ROWS // ATT_PAIR
    m0_top = tuple(sets[1 + p][4] for p in range(npair))
    m0_bot = tuple(sets[1 + npair + p][4] for p in range(npair))

    def kmap(t):
        def f(bi, j, i):
            c0 = jnp.clip(i * (ATT_ROWS // chunk_rows) - (WIN_H // 2) // chunk_rows, 0,
                          (rows - ATT_KROWS) // chunk_rows)
            return (bi, c0 + t, 0, j)
        return f

    kv_specs = [pl.BlockSpec((1, 1, ck, LANES), kmap(t)) for t in range(nchunk)]
    return pl.pallas_call(
        functools.partial(_attn_kernel, m0_top=m0_top, m0_bot=m0_bot, bounded=bounded),
        grid=(b, d // LANES, nt),
        in_specs=[pl.BlockSpec((1, tq, LANES), lambda bi, j, i: (bi, i, j))]
                 + kv_specs + kv_specs
                 + [pl.BlockSpec((1, c, LANES), lambda bi, j, i: (bi, 0, j)),
                    pl.BlockSpec((1, c, LANES), lambda bi, j, i: (bi, 0, j)),
                    pl.BlockSpec((1,) + bias_tab.shape[1:], lambda bi, j, i: (j, 0, 0, 0, 0))],
        out_specs=pl.BlockSpec((1, tq, LANES), lambda bi, j, i: (bi, i, j)),
        out_shape=jax.ShapeDtypeStruct((b, s, d), BF16),
        scratch_shapes=[pltpu.VMEM((ATT_KROWS * GRID_W, LANES), BF16),
                        pltpu.VMEM((ATT_KROWS * GRID_W, vlanes), BF16),
                        pltpu.VMEM((c, vlanes), BF16)],
        compiler_params=_cparams(("parallel", "parallel", "arbitrary")),
        name="nbr_attention_bounded" if bounded else "nbr_attention",
    )(q, *([k4] * nchunk), *([v4] * nchunk), kc, vc, bias_tab)


def _ctx_attn_kernel(q_ref, k_ref, v_ref, o_ref):
    lane = lax.broadcasted_iota(jnp.int32, (1, LANES), 1)
    head_lo = lane < (LANES // 2)
    q = q_ref[0]
    k = k_ref[0]
    v = v_ref[0]
    outs = []
    for hh in range(2):
        mask = head_lo if hh == 0 else jnp.logical_not(head_lo)
        qh = jnp.where(mask, q, jnp.zeros_like(q))
        s = lax.dot_general(qh, k, (((1,), (1,)), ((), ())), preferred_element_type=F32)
        m = jnp.max(s, axis=-1, keepdims=True)
        e = jnp.exp(s - m)
        den = jnp.sum(e, axis=-1, keepdims=True)
        o = jnp.dot(e.astype(BF16), v, preferred_element_type=F32)
        outs.append(o * (1.0 / den))
    o_ref[0] = jnp.where(head_lo, outs[0], outs[1]).astype(BF16)


def _ctx_attention(q, k, v):
    b, c, d = q.shape
    spec = pl.BlockSpec((1, c, LANES), lambda bi, j: (bi, 0, j))
    return pl.pallas_call(
        _ctx_attn_kernel,
        grid=(b, d // LANES),
        in_specs=[spec, spec, spec],
        out_specs=spec,
        out_shape=jax.ShapeDtypeStruct((b, c, d), BF16),
        compiler_params=_cparams(("parallel", "parallel")),
        name="ctx_attention",
    )(q, k, v)


def _swiglu_chunks(h, w1_at, w3_at, w2_at, ff, tf):
    acc = None
    for c0 in range(0, ff, tf):
        a = jnp.dot(h, w1_at(c0, tf), preferred_element_type=F32)
        b = jnp.dot(h, w3_at(c0, tf), preferred_element_type=F32)
        t = (a * jax.nn.sigmoid(a) * b).astype(BF16)
        y = jnp.dot(t, w2_at(c0, tf), preferred_element_type=F32)
        acc = y if acc is None else acc + y
    return acc


def _attnproj_ffn_kernel(a_ref, wo_ref, x_ref, gate_a_ref, g_ref, sh_ref, sc_ref, gate_ref,
                         w1_ref, w3_ref, w2_ref, o_ref, *, tf):
    x1 = x_ref[0] + gate_a_ref[0] * jnp.dot(a_ref[0], wo_ref[...], preferred_element_type=F32)
    h = _rms_mod(x1, g_ref[...], sh_ref[0], sc_ref[0]).astype(BF16)
    y = _swiglu_chunks(h, lambda c0, n: w1_ref[:, c0:c0 + n], lambda c0, n: w3_ref[:, c0:c0 + n],
                       lambda c0, n: w2_ref[c0:c0 + n, :], w1_ref.shape[1], tf)
    o_ref[0] = x1 + gate_ref[0] * y


def _attnproj_ffn(a, wo_bf, x, gate_a, norm_g, shift, scale, gate, w1_bf, w3_bf, w2_bf, tm, tf):
    bx, sx, d = x.shape
    ff = w1_bf.shape[1]
    row = lambda b, i: (b, i, 0)
    vec = lambda b, i: (b, 0, 0)
    const2 = lambda b, i: (0, 0)
    once = pl.Buffered(1)
    return pl.pallas_call(
        functools.partial(_attnproj_ffn_kernel, tf=tf),
        grid=(bx, sx // tm),
        in_specs=[pl.BlockSpec((1, tm, d), row),
                  pl.BlockSpec((d, d), const2, pipeline_mode=once),
                  pl.BlockSpec((1, tm, d), row),
                  pl.BlockSpec((1, 1, d), vec),
                  pl.BlockSpec((1, d), const2),
                  pl.BlockSpec((1, 1, d), vec),
                  pl.BlockSpec((1, 1, d), vec),
                  pl.BlockSpec((1, 1, d), vec),
                  pl.BlockSpec((d, ff), const2, pipeline_mode=once),
                  pl.BlockSpec((d, ff), const2, pipeline_mode=once),
                  pl.BlockSpec((ff, d), const2, pipeline_mode=once)],
        out_specs=pl.BlockSpec((1, tm, d), row),
        out_shape=jax.ShapeDtypeStruct(x.shape, F32),
        compiler_params=_cparams(("parallel", "parallel")),
        name="attnproj_ffn",
    )(a, wo_bf, x, gate_a, norm_g.reshape(1, d), shift, scale, gate, w1_bf, w3_bf, w2_bf)


def _inproj_kernel(x_ref, xp_ref, xn_ref, g_ref, sh_ref, sc_ref, w_ref, cw_ref, cb_ref, gb_ref, xc_ref, ext,
                   *, dr):
    i = pl.program_id(1)
    nb = pl.num_programs(1)
    tm = x_ref.shape[1]
    pad = SUBLANES
    xe = jnp.concatenate([xp_ref[0], x_ref[0], xn_ref[0]], axis=0)
    h = _rms_mod(xe, g_ref[...], sh_ref[0], sc_ref[0]).astype(BF16)
    rowi = lax.broadcasted_iota(jnp.int32, (tm + 2 * pad, 1), 0)
    inside = jnp.logical_and(jnp.logical_or(rowi >= pad, i > 0),
                             jnp.logical_or(rowi < tm + pad, i < nb - 1))
    xr = jnp.dot(h, w_ref[:, dr:], preferred_element_type=F32)
    ext[...] = jnp.where(inside, xr, 0.0)
    cw = cw_ref[...]
    xc = cb_ref[...] + sum(
        cw[j:j + 1, :] * ext[pad + j - CONV_LEFT:pad + j - CONV_LEFT + tm, :] for j in range(cw.shape[0]))
    for k in range(dr // LANES):
        xc_ref[0, k] = xc[:, k * LANES:(k + 1) * LANES]
    gb_ref[0] = jnp.dot(h, w_ref[:, :dr], preferred_element_type=F32)[pad:pad + tm]


def _inproj(x, norm_g, shift, scale, w_in_bf, conv_w, conv_b, tm):
    bx, sx, d = x.shape
    dr = w_in_bf.shape[1] // 2
    tb = tm // SUBLANES
    row = lambda b, i: (b, i, 0)
    vec = lambda b, i: (b, 0, 0)
    const2 = lambda b, i: (0, 0)
    out = jax.ShapeDtypeStruct((bx, sx, dr), F32)
    return pl.pallas_call(
        functools.partial(_inproj_kernel, dr=dr),
        grid=(bx, sx // tm),
        in_specs=[pl.BlockSpec((1, tm, d), row),
                  pl.BlockSpec((1, SUBLANES, d), lambda b, i: (b, jnp.maximum(i * tb - 1, 0), 0)),
                  pl.BlockSpec((1, SUBLANES, d),
                               lambda b, i: (b, jnp.minimum((i + 1) * tb, sx // SUBLANES - 1), 0)),
                  pl.BlockSpec((1, d), const2),
                  pl.BlockSpec((1, 1, d), vec),
                  pl.BlockSpec((1, 1, d), vec),
                  pl.BlockSpec((d, 2 * dr), const2),
                  pl.BlockSpec(conv_w.shape, const2),
                  pl.BlockSpec((1, dr), const2)],
        out_specs=[pl.BlockSpec((1, tm, dr), row),
                   pl.BlockSpec((1, dr // LANES, tm, LANES), lambda b, i: (b, 0, i, 0))],
        out_shape=[out, jax.ShapeDtypeStruct((bx, dr // LANES, sx, LANES), F32)],
        scratch_shapes=[pltpu.VMEM((tm + 2 * SUBLANES, dr), F32)],
        compiler_params=_cparams(("parallel", "parallel")),
        name="rglru_inproj",
    )(x, x, x, norm_g.reshape(1, d), shift, scale, w_in_bf, conv_w, conv_b.reshape(1, dr))


def _lru_scan_kernel(xc_hbm, wa_ref, wx_ref, ba_ref, bx_ref, lam_ref, h0_ref, o_hbm, xbuf, obuf, carry,
                     sem_in, sem_out, *, reverse, ngroups):
    bi = pl.program_id(0)
    i = pl.program_id(1)
    nb = pl.num_programs(1)
    ncg, nl = xbuf.shape[1], xbuf.shape[2]
    t = nl * SUBLANES
    c = ncg * LANES
    slot = i % 2

    def block_start(step):
        return ((nb - 1 - step) if reverse else step) * t

    def in_copies(step, to_slot):
        return [pltpu.make_async_copy(xc_hbm.at[bi, :, pl.ds(block_start(step) + ch * nl, nl), :],
                                      xbuf.at[to_slot, :, :, ch, :], sem_in.at[to_slot])
                for ch in range(SUBLANES)]

    def out_copies(step, from_slot):
        return [pltpu.make_async_copy(obuf.at[from_slot, :, :, ch, :],
                                      o_hbm.at[bi, :, pl.ds(block_start(step) + ch * nl, nl), :],
                                      sem_out.at[from_slot])
                for ch in range(SUBLANES)]

    @pl.when(i == 0)
    def _():
        carry[...] = jnp.broadcast_to(h0_ref[0], carry.shape)
        for cp in in_copies(0, 0):
            cp.start()

    @pl.when(i + 1 < nb)
    def _():
        for cp in in_copies(i + 1, 1 - slot):
            cp.start()

    for cp in in_copies(i, slot):
        cp.wait()

    @pl.when(i >= 2)
    def _():
        for cp in out_copies(i, slot):
            cp.wait()

    gw = c // ngroups
    kpg = gw // LANES
    steps = range(nl - 1, -1, -1) if reverse else range(nl)
    rows = lambda v, j: v[j * SUBLANES:(j + 1) * SUBLANES]
    sub = lax.broadcasted_iota(jnp.int32, (SUBLANES, 1), 0)
    for gi in range(ngroups):
        cs = slice(gi * gw, (gi + 1) * gw)
        xc = jnp.concatenate([xbuf[slot, gi * kpg + k].reshape(t, LANES) for k in range(kpg)], axis=1)
        xb = xc.astype(BF16)
        t_a = jnp.tanh(jnp.dot(xb, wa_ref[gi], preferred_element_type=F32) + 0.5 * ba_ref[:, cs])
        t_x = jnp.tanh(jnp.dot(xb, wx_ref[gi], preferred_element_type=F32) + 0.5 * bx_ref[:, cs])
        lam = lam_ref[:, cs]
        sp = jnp.maximum(-lam, 0.0) + jnp.log(1.0 + jnp.exp(-jnp.abs(lam)))
        k1 = ((-0.5 * LRU_C) * np.float32(np.log2(np.e))) * sp
        a = jnp.exp2(t_a * k1 + k1)
        om = 1.0 - a * a
        u = (om * lax.rsqrt(jnp.maximum(om, 1e-37))) * ((0.5 * t_x + 0.5) * xc)

        hloc = jnp.zeros((SUBLANES, gw), F32)
        aprod = jnp.ones((SUBLANES, gw), F32)
        for j in steps:
            hloc = rows(a, j) * hloc + rows(u, j)
            aprod = rows(a, j) * aprod

        hprev = carry[:, cs]
        for sft in (1, 2, 4):
            if reverse:
                a_sh = pltpu.roll(aprod, SUBLANES - sft, axis=0)
                h_sh = pltpu.roll(hloc, SUBLANES - sft, axis=0)
                ok = sub < SUBLANES - sft
            else:
                a_sh = pltpu.roll(aprod, sft, axis=0)
                h_sh = pltpu.roll(hloc, sft, axis=0)
                ok = sub >= sft
            hloc = jnp.where(ok, aprod * h_sh + hloc, hloc)
            aprod = jnp.where(ok, aprod * a_sh, aprod)
        after = hloc + aprod * hprev
        if reverse:
            h = jnp.where(sub == SUBLANES - 1, hprev, pltpu.roll(after, SUBLANES - 1, axis=0))
            carry[:, cs] = jnp.broadcast_to(after[0:1], after.shape)
        else:
            h = jnp.where(sub == 0, hprev, pltpu.roll(after, 1, axis=0))
            carry[:, cs] = jnp.broadcast_to(after[SUBLANES - 1:SUBLANES], after.shape)

        for j in steps:
            h = rows(a, j) * h + rows(u, j)
            for k in range(kpg):
                obuf[slot, gi * kpg + k, j] = h[:, k * LANES:(k + 1) * LANES]
    for cp in out_copies(i, slot):
        cp.start()

    @pl.when(i == nb - 1)
    def _():
        for cp in out_copies(i, slot):
            cp.wait()

    @pl.when(jnp.logical_and(i == nb - 1, nb >= 2))
    def _():
        for cp in out_copies(i, 1 - slot):
            cp.wait()


def _lru_scan(xc, wa_g, wx_g, ba, bx, lam, h0, reverse, t):
    b, ncg, s, _ = xc.shape
    c = ncg * LANES
    nb = s // t
    nl = t // SUBLANES
    ngroups = wa_g.shape[0]
    gw = c // ngroups
    const2 = lambda bi, i: (0, 0)
    const3 = lambda bi, i: (0, 0, 0)
    return pl.pallas_call(
        functools.partial(_lru_scan_kernel, reverse=reverse, ngroups=ngroups),
        grid=(b, nb),
        in_specs=[pl.BlockSpec(memory_space=pl.ANY),
                  pl.BlockSpec((ngroups, gw, gw), const3),
                  pl.BlockSpec((ngroups, gw, gw), const3),
                  pl.BlockSpec((1, c), const2),
                  pl.BlockSpec((1, c), const2),
                  pl.BlockSpec((1, c), const2),
                  pl.BlockSpec((1, 1, c), lambda bi, i: (bi, 0, 0))],
        out_specs=pl.BlockSpec(memory_space=pl.ANY),
        out_shape=jax.ShapeDtypeStruct((b, ncg, s, LANES), F32),
        scratch_shapes=[pltpu.VMEM((2, ncg, nl, SUBLANES, LANES), F32),
                        pltpu.VMEM((2, ncg, nl, SUBLANES, LANES), F32),
                        pltpu.VMEM((SUBLANES, c), F32),
                        pltpu.SemaphoreType.DMA((2,)),
                        pltpu.SemaphoreType.DMA((2,))],
        compiler_params=_cparams(("parallel", "arbitrary")),
        name="lru_scan_bwd" if reverse else "lru_scan_fwd",
    )(xc, wa_g, wx_g, ba.reshape(1, c), bx.reshape(1, c), lam.reshape(1, c), h0)


def _lru_out_router_kernel(g_ref, hf_ref, hb_ref, w_ref, x_ref, gate_ref, ng_ref, sh_ref, sc_ref,
                           wh_ref, wl_ref, br_ref, o_ref, hp_ref, rt_ref, *, n_exp):
    hs = jnp.concatenate([hf_ref[0, k] + hb_ref[0, k] for k in range(hf_ref.shape[1])], axis=1)
    a = (jax.nn.gelu(g_ref[0]) * hs).astype(BF16)
    x1 = x_ref[0] + gate_ref[0] * jnp.dot(a, w_ref[...], preferred_element_type=F32)
    o_ref[0] = x1
    h = _rms_mod(x1, ng_ref[...], sh_ref[0], sc_ref[0])
    hb = h.astype(BF16)
    hf = hb.astype(F32)
    h_lo = (h - hf).astype(BF16)
    logits = (jnp.dot(hb, wh_ref[...], preferred_element_type=F32)
              + jnp.dot(hb, wl_ref[...], preferred_element_type=F32)
              + jnp.dot(h_lo, wh_ref[...], preferred_element_type=F32)) + br_ref[...]
    lane = lax.broadcasted_iota(jnp.int32, logits.shape, 1)
    logits = jnp.where(lane < n_exp, logits, -jnp.inf)
    m1 = jnp.max(logits, axis=-1, keepdims=True)
    i1 = jnp.min(jnp.where(logits == m1, lane, LANES), axis=-1, keepdims=True)
    rest = jnp.where(lane == i1, -jnp.inf, logits)
    m2 = jnp.max(rest, axis=-1, keepdims=True)
    i2 = jnp.min(jnp.where(rest == m2, lane, LANES), axis=-1, keepdims=True)
    e2 = jnp.exp(m2 - m1)
    g1 = 1.0 / (1.0 + e2)
    g2 = e2 * g1
    rt_ref[0] = jnp.where(lane == 0, g1, jnp.where(lane == 1, g2, jnp.where(
        lane == 2, i1.astype(F32), jnp.where(lane == 3, i2.astype(F32), 0.0))))
    hp_ref[0] = h


def _lru_out_router(g, hf, hb, w_out_bf, x, gate, norm_g, shift, scale, w_router, b_router, tm):
    bx, sx, dr = g.shape
    d = x.shape[-1]
    n_exp = w_router.shape[1]
    wpad = jnp.zeros((d, LANES), F32).at[:, :n_exp].set(w_router)
    w_hi = wpad.astype(BF16)
    w_lo = (wpad - w_hi.astype(F32)).astype(BF16)
    bpad = jnp.zeros((1, LANES), F32).at[0, :n_exp].set(b_router)
    row = lambda b, i: (b, i, 0)
    vec = lambda b, i: (b, 0, 0)
    const2 = lambda b, i: (0, 0)
    cg = pl.BlockSpec((1, dr // LANES, tm, LANES), lambda b, i: (b, 0, i, 0))
    return pl.pallas_call(
        functools.partial(_lru_out_router_kernel, n_exp=n_exp),
        grid=(bx, sx // tm),
        in_specs=[pl.BlockSpec((1, tm, dr), row), cg, cg,
                  pl.BlockSpec((dr, d), const2),
                  pl.BlockSpec((1, tm, d), row),
                  pl.BlockSpec((1, 1, d), vec),
                  pl.BlockSpec((1, d), const2),
                  pl.BlockSpec((1, 1, d), vec),
                  pl.BlockSpec((1, 1, d), vec),
                  pl.BlockSpec((d, LANES), const2),
                  pl.BlockSpec((d, LANES), const2),
                  pl.BlockSpec((1, LANES), const2)],
        out_specs=[pl.BlockSpec((1, tm, d), row), pl.BlockSpec((1, tm, d), row),
                   pl.BlockSpec((1, tm, LANES), row)],
        out_shape=[jax.ShapeDtypeStruct(x.shape, F32), jax.ShapeDtypeStruct(x.shape, F32),
                   jax.ShapeDtypeStruct((bx, sx, LANES), F32)],
        compiler_params=_cparams(("parallel", "parallel")),
        name="rglru_outproj_router",
    )(g, hf, hb, w_out_bf, x, gate, norm_g.reshape(1, d), shift, scale, w_hi, w_lo, bpad)


def _dispatch_kernel(dest_ref, fill_ref, hp_ref, xs_ref, zrow, sem, zsem, *, td, n_fill):
    base = pl.program_id(0) * td

    @pl.when(pl.program_id(0) == 0)
    def _():
        zrow[...] = jnp.zeros_like(zrow)

        def zissue(r, _):
            pltpu.make_async_copy(zrow.at[pl.ds(0, 1)], xs_ref.at[pl.ds(fill_ref[r], 1)], zsem).start()
            return 0

        lax.fori_loop(0, n_fill, zissue, 0, unroll=8)

    def issue(r, _):
        for kk in range(TOP_K):
            dst = dest_ref[(base + r) * TOP_K + kk]
            pltpu.make_async_copy(hp_ref.at[pl.ds(r, 1)], xs_ref.at[pl.ds(dst, 1)], sem).start()
        return 0

    lax.fori_loop(0, td, issue, 0, unroll=8)
    for kk in range(TOP_K):
        pltpu.make_async_copy(hp_ref, xs_ref.at[pl.ds(0, td)], sem).wait()

    @pl.when(pl.program_id(0) == 0)
    def _():
        pltpu.make_async_copy(xs_ref.at[pl.ds(0, n_fill)], xs_ref.at[pl.ds(0, n_fill)], zsem).wait()


def _dispatch(hp, dest, fill, n_rows, td):
    n, w = hp.shape
    n_fill = fill.shape[0]
    return pl.pallas_call(
        functools.partial(_dispatch_kernel, td=td, n_fill=n_fill),
        grid_spec=pltpu.PrefetchScalarGridSpec(
            num_scalar_prefetch=2,
            grid=(n // td,),
            in_specs=[pl.BlockSpec((td, w), lambda i, dref, fref: (i, 0))],
            out_specs=pl.BlockSpec(memory_space=pl.ANY),
            scratch_shapes=[pltpu.VMEM((SUBLANES, w), hp.dtype),
                            pltpu.SemaphoreType.DMA(()), pltpu.SemaphoreType.DMA(())]),
        out_shape=jax.ShapeDtypeStruct((n_rows, w), hp.dtype),
        compiler_params=pltpu.CompilerParams(dimension_semantics=("arbitrary",),
                                             vmem_limit_bytes=VMEM_LIMIT, has_side_effects=True),
        name="moe_dispatch",
    )(dest, fill, hp)


def _experts_kernel(te_ref, nu_ref, x_ref, w1_ref, w3_ref, w2_ref, o_ref, *, tf):
    del te_ref
    i = pl.program_id(0)

    @pl.when(i < nu_ref[0])
    def _():
        o_ref[...] = _swiglu_chunks(
            x_ref[...].astype(BF16), lambda c0, n: w1_ref[0, :, c0:c0 + n],
            lambda c0, n: w3_ref[0, :, c0:c0 + n], lambda c0, n: w2_ref[0, c0:c0 + n, :],
            w1_ref.shape[2], tf)

    @pl.when(i >= nu_ref[0])
    def _():
        o_ref[...] = jnp.zeros_like(o_ref)


def _experts(xs, tile_expert, n_used, w1_bf, w3_bf, w2_bf, tm, tf):
    n_rows, w = xs.shape
    n_exp, d, ff = w1_bf.shape
    n_tiles = n_rows // tm

    def live(i, nu):
        return jnp.minimum(i, nu[0] - 1)

    once = pl.Buffered(1)
    return pl.pallas_call(
        functools.partial(_experts_kernel, tf=tf),
        grid_spec=pltpu.PrefetchScalarGridSpec(
            num_scalar_prefetch=2,
            grid=(n_tiles,),
            in_specs=[pl.BlockSpec((tm, w), lambda i, te, nu: (live(i, nu), 0)),
                      pl.BlockSpec((1, d, ff), lambda i, te, nu: (te[live(i, nu)], 0, 0), pipeline_mode=once),
                      pl.BlockSpec((1, d, ff), lambda i, te, nu: (te[live(i, nu)], 0, 0), pipeline_mode=once),
                      pl.BlockSpec((1, ff, d), lambda i, te, nu: (te[live(i, nu)], 0, 0), pipeline_mode=once)],
            out_specs=pl.BlockSpec((tm, d), lambda i, te, nu: (i, 0))),
        out_shape=jax.ShapeDtypeStruct((n_rows, d), F32),
        compiler_params=_cparams(("arbitrary",)),
        name="moe_experts",
    )(tile_expert, n_used, xs, w1_bf, w3_bf, w2_bf)


def _combine_kernel(pos_ref, x_ref, gate_ref, rt_ref, y_ref, o_ref, ybuf, sem, *, tc):
    i = pl.program_id(0)
    slot = i % 2

    def issue(tile, to_slot):
        base = tile * tc

        def body(r, _):
            for kk in range(TOP_K):
                src = pos_ref[(base + r) * TOP_K + kk]
                pltpu.make_async_copy(y_ref.at[pl.ds(src, 1)], ybuf.at[to_slot, kk, pl.ds(r, 1)],
                                      sem.at[to_slot]).start()
            return 0

        lax.fori_loop(0, tc, body, 0, unroll=8)

    @pl.when(i == 0)
    def _():
        issue(0, 0)

    @pl.when(i + 1 < pl.num_programs(0))
    def _():
        issue(i + 1, 1 - slot)

    for kk in range(TOP_K):
        pltpu.make_async_copy(y_ref.at[pl.ds(0, tc)], ybuf.at[slot, kk], sem.at[slot]).wait()
    rt = rt_ref[...]
    y = rt[:, 0:1] * ybuf[slot, 0] + rt[:, 1:2] * ybuf[slot, 1]
    o_ref[...] = x_ref[...] + gate_ref[0] * y


def _combine(x2, gate, route, y, pos, seq, tc):
    n, d = x2.shape
    return pl.pallas_call(
        functools.partial(_combine_kernel, tc=tc),
        grid_spec=pltpu.PrefetchScalarGridSpec(
            num_scalar_prefetch=1,
            grid=(n // tc,),
            in_specs=[pl.BlockSpec((tc, d), lambda i, p: (i, 0)),
                      pl.BlockSpec((1, 1, d), lambda i, p: ((i * tc) // seq, 0, 0)),
                      pl.BlockSpec((tc, LANES), lambda i, p: (i, 0)),
                      pl.BlockSpec(memory_space=pl.ANY)],
            out_specs=pl.BlockSpec((tc, d), lambda i, p: (i, 0)),
            scratch_shapes=[pltpu.VMEM((2, TOP_K, tc, d), F32), pltpu.SemaphoreType.DMA((2,))]),
        out_shape=jax.ShapeDtypeStruct((n, d), F32),
        compiler_params=_cparams(("arbitrary",)),
        name="moe_combine",
    )(pos, x2, gate, route, y)


def _route_positions(e_flat, n_exp, tm):
    a = e_flat.shape[0]
    onehot = (e_flat[:, None] == jnp.arange(n_exp, dtype=jnp.int32)[None, :]).astype(jnp.int32)
    csum = jnp.cumsum(onehot, axis=0)
    counts = csum[-1]
    padded = ((counts + tm - 1) // tm) * tm
    ends = jnp.cumsum(padded)
    starts = ends - padded
    dest = jnp.sum(onehot * (csum - 1 + starts[None, :]), axis=1).astype(jnp.int32)
    n_tiles = a // tm + n_exp
    tile_ids = jnp.arange(n_tiles, dtype=jnp.int32)
    tile_expert = jnp.minimum(jnp.sum((ends[None, :] // tm <= tile_ids[:, None]).astype(jnp.int32), axis=1),
                              n_exp - 1)
    n_used = (ends[-1] // tm).astype(jnp.int32).reshape(1)
    j = jnp.arange(tm, dtype=jnp.int32)[None, :]
    is_pad = counts[:, None] + j < padded[:, None]
    tail_rank = jnp.cumsum((~is_pad).astype(jnp.int32).reshape(-1)) - 1
    fill = jnp.where(is_pad.reshape(-1), (starts[:, None] + counts[:, None] + j).reshape(-1),
                     ends[-1] + tail_rank).astype(jnp.int32)
    return dest, fill, tile_expert.astype(jnp.int32), n_used, n_tiles * tm


def _blockdiag_groups(w, ngroups):
    nblk, bw, _ = w.shape
    per = nblk // ngroups
    eye = jnp.eye(per, dtype=w.dtype)
    wg = w.reshape(ngroups, per, bw, bw)
    full = wg[:, :, :, None, :] * eye[None, :, None, :, None]
    return full.reshape(ngroups, per * bw, per * bw)


def kernel(x, c, ctx, c_ctx, l0_w_mod, l0_b_mod, l0_norm1, l0_norm2, l0_w_qkv, l0_q_gain, l0_k_gain, l0_rpb, l0_w_o, l0_ffn_w1, l0_ffn_w3, l0_ffn_w2, l1_w_mod, l1_b_mod, l1_norm1, l1_norm2, l1_w_in, l1_conv_w, l1_conv_b, l1_gate_a_w, l1_gate_a_b, l1_gate_x_w, l1_gate_x_b, l1_lam, l1_w_out, l1_router_w, l1_router_b, l1_moe_w1, l1_moe_w3, l1_moe_w2):
    b, s, d = x.shape
    n_ctx = ctx.shape[1]
    head_dim = l0_q_gain.shape[0]
    n_heads = d // head_dim
    n_exp = l1_router_w.shape[1]
    tm = min(512, s)
    tmc = min(512, n_ctx)

    def mods(w_mod, b_mod):
        m = _adaln_mod(jnp.concatenate([c, c_ctx[None, :]], axis=0), w_mod, b_mod)
        lat = [m[:b, j * d:(j + 1) * d].reshape(b, 1, d) for j in range(6)]
        cx = [jnp.broadcast_to(m[b:b + 1, j * d:(j + 1) * d].reshape(1, 1, d), (b, 1, d)) for j in range(6)]
        return lat, cx

    ml, mc = mods(l0_w_mod, l0_b_mod)
    w_qkv = l0_w_qkv.astype(BF16)
    qg = (jnp.tile(l0_q_gain, n_heads) * (head_dim ** -0.5)).reshape(1, d)
    kg = jnp.tile(l0_k_gain, n_heads).reshape(1, d)
    ql, kl, vl = _qkv(x, l0_norm1, ml[0], ml[1], w_qkv, qg, kg, head_dim, tm)
    qc, kc, vc = _qkv(ctx, l0_norm1, mc[0], mc[1], w_qkv, qg, kg, head_dim, tmc)
    logit_bound = (1.02 * head_dim) * jnp.max(jnp.abs(qg)) * jnp.max(jnp.abs(kg)) + jnp.max(jnp.abs(l0_rpb))
    bias_tab = _att_bias_table(l0_rpb)
    o_l = lax.cond(logit_bound <= ATT_LOGIT_BOUND,
                   lambda: _attention(ql, kl, vl, kc, vc, bias_tab, True),
                   lambda: _attention(ql, kl, vl, kc, vc, bias_tab, False))
    o_c = _ctx_attention(qc, kc, vc)
    w_o = l0_w_o.astype(BF16)
    w1, w3, w2 = l0_ffn_w1.astype(BF16), l0_ffn_w3.astype(BF16), l0_ffn_w2.astype(BF16)
    tf0 = l0_ffn_w1.shape[1]
    x_l = _attnproj_ffn(o_l, w_o, x, ml[2], l0_norm2, ml[3], ml[4], ml[5], w1, w3, w2, tm, tf0)
    x_c = _attnproj_ffn(o_c, w_o, ctx, mc[2], l0_norm2, mc[3], mc[4], mc[5], w1, w3, w2, tmc, tf0)

    ml, mc = mods(l1_w_mod, l1_b_mod)
    w_in = l1_w_in.astype(BF16)
    g_l, xr_l = _inproj(x_l, l1_norm1, ml[0], ml[1], w_in, l1_conv_w, l1_conv_b, tm)
    _, xr_c = _inproj(x_c, l1_norm1, mc[0], mc[1], w_in, l1_conv_w, l1_conv_b, tmc)
    dr = g_l.shape[-1]
    ngroups = 4
    t_scan = min(256, n_ctx)
    zeros0 = jnp.zeros((b, 1, dr), F32)
    hs = []
    for di, rev in enumerate((False, True)):
        wa_g = (0.5 * _blockdiag_groups(l1_gate_a_w[di], ngroups)).astype(BF16)
        wx_g = (0.5 * _blockdiag_groups(l1_gate_x_w[di], ngroups)).astype(BF16)
        args = (wa_g, wx_g, l1_gate_a_b[di], l1_gate_x_b[di], l1_lam[di])
        h_c = _lru_scan(xr_c, *args, zeros0, rev, t_scan)
        h0 = (h_c[:, :, 0, :] if rev else h_c[:, :, n_ctx - 1, :]).reshape(b, 1, dr)
        hs.append(_lru_scan(xr_l, *args, h0, rev, t_scan))
    x_l, hp, route = _lru_out_router(g_l, hs[0], hs[1], l1_w_out.astype(BF16), x_l, ml[2],
                                     l1_norm2, ml[3], ml[4], l1_router_w, l1_router_b, tm)
    n = b * s
    route2 = route.reshape(n, LANES)
    e_flat = route2[:, 2:2 + TOP_K].astype(jnp.int32).reshape(n * TOP_K)
    tme = 512
    dest, fill, tile_expert, n_used, n_rows = _route_positions(e_flat, n_exp, tme)
    xs = _dispatch(hp.reshape(n, d), dest, fill, n_rows, min(512, n))
    y = _experts(xs, tile_expert, n_used, l1_moe_w1.astype(BF16), l1_moe_w3.astype(BF16),
                 l1_moe_w2.astype(BF16), tme, l1_moe_w1.shape[2] // 2)
    out = _combine(x_l.reshape(n, d), ml[5], route2, y, dest, s, min(256, n))
    return out.reshape(b, s, d)
```

```python
import functools

import numpy as np
import jax
import jax.numpy as jnp
from jax import lax
from jax.experimental import pallas as pl
from jax.experimental.pallas import tpu as pltpu

GRID_W = 64
WIN_H = 8
WIN_W = 16
TOP_K = 2
LRU_C = 8.0
EPS = 1e-6
CONV_LEFT = 2
NEG_BIG = -1e30
LANES = 128
SUBLANES = 8
VMEM_LIMIT = 56 * 1024 * 1024

ATT_ROWS = 8
ATT_KROWS = 16
ATT_PAIR = 2
ATT_BAND = 10
ATT_LOGIT_BOUND = 40.0

F32 = jnp.float32
BF16 = jnp.bfloat16


def _cparams(sem):
    return pltpu.CompilerParams(dimension_semantics=sem, vmem_limit_bytes=VMEM_LIMIT)


def _rms_mod(x, g, shift, scale):
    ms = jnp.mean(x * x, axis=-1, keepdims=True)
    y = x * lax.rsqrt(ms + EPS) * g
    return y * (1.0 + scale) + shift


def _mod_kernel(ct_ref, w_ref, b_ref, o_ref, *, nrows):
    ct = ct_ref[...]
    s = ct * jax.nn.sigmoid(ct)
    w = w_ref[...]
    rows = [jnp.sum(w * s[:, m:m + 1], axis=0, keepdims=True) + b_ref[...] for m in range(nrows)]
    rows += [jnp.zeros_like(rows[0])] * (SUBLANES - nrows)
    o_ref[...] = jnp.concatenate(rows, axis=0)


def _adaln_mod(cvecs, w_mod, b_mod):
    nrows, d = cvecs.shape
    n = w_mod.shape[1]
    tn = 768
    ct = jnp.zeros((d, SUBLANES), F32).at[:, :nrows].set(cvecs.T)
    return pl.pallas_call(
        functools.partial(_mod_kernel, nrows=nrows),
        grid=(n // tn,),
        in_specs=[pl.BlockSpec((d, SUBLANES), lambda j: (0, 0)),
                  pl.BlockSpec((d, tn), lambda j: (0, j)),
                  pl.BlockSpec((1, tn), lambda j: (0, j))],
        out_specs=pl.BlockSpec((SUBLANES, tn), lambda j: (0, j)),
        out_shape=jax.ShapeDtypeStruct((SUBLANES, n), F32),
        compiler_params=_cparams(("arbitrary",)),
        name="adaln_mod",
    )(ct, w_mod, b_mod.reshape(1, n))


def _qkv_kernel(x_ref, g_ref, sh_ref, sc_ref, w_ref, qg_ref, kg_ref, pm_ref, pe_ref,
                q_ref, k_ref, v_ref, *, d):
    h = _rms_mod(x_ref[0], g_ref[...], sh_ref[0], sc_ref[0]).astype(BF16)
    for part, (gain_ref, out_ref) in enumerate(((qg_ref, q_ref), (kg_ref, k_ref))):
        y = jnp.dot(h, w_ref[:, part * d:(part + 1) * d], preferred_element_type=F32)
        ms = jnp.dot((y * y).astype(BF16), pm_ref[...], preferred_element_type=F32)
        inv = lax.rsqrt(ms + EPS)
        inv_hi = inv.astype(BF16)
        inv_lo = (inv - inv_hi.astype(F32)).astype(BF16)
        full = (jnp.dot(inv_hi, pe_ref[...], preferred_element_type=F32)
                + jnp.dot(inv_lo, pe_ref[...], preferred_element_type=F32))
        out_ref[0] = (y * full * gain_ref[...]).astype(BF16)
    v_ref[0] = jnp.dot(h, w_ref[:, 2 * d:3 * d], preferred_element_type=F32).astype(BF16)


def _qkv(x, norm_g, shift, scale, w_qkv_bf, q_gain_full, k_gain_full, head_dim, tm):
    bx, sx, d = x.shape
    n_heads = d // head_dim
    head_of = np.arange(d) // head_dim
    pm = np.zeros((d, LANES), np.float32)
    pm[np.arange(d), head_of] = 1.0 / head_dim
    pe = np.zeros((LANES, d), np.float32)
    pe[head_of, np.arange(d)] = 1.0
    assert n_heads <= LANES
    row = lambda b, i: (b, i, 0)
    vec = lambda b, i: (b, 0, 0)
    const2 = lambda b, i: (0, 0)
    out = jax.ShapeDtypeStruct((bx, sx, d), BF16)
    return pl.pallas_call(
        functools.partial(_qkv_kernel, d=d),
        grid=(bx, sx // tm),
        in_specs=[pl.BlockSpec((1, tm, d), row),
                  pl.BlockSpec((1, d), const2),
                  pl.BlockSpec((1, 1, d), vec),
                  pl.BlockSpec((1, 1, d), vec),
                  pl.BlockSpec((d, 3 * d), const2),
                  pl.BlockSpec((1, d), const2),
                  pl.BlockSpec((1, d), const2),
                  pl.BlockSpec((d, LANES), const2),
                  pl.BlockSpec((LANES, d), const2)],
        out_specs=[pl.BlockSpec((1, tm, d), row)] * 3,
        out_shape=[out, out, out],
        compiler_params=_cparams(("parallel", "parallel")),
        name="qkv_headnorm",
    )(x, norm_g.reshape(1, d), shift, scale, w_qkv_bf, q_gain_full, k_gain_full,
      jnp.asarray(pm, BF16), jnp.asarray(pe, BF16))


def _att_band_layout(rows):
    npair = ATT_ROWS // ATT_PAIR
    half_win = WIN_H // 2
    sets = []
    for kind, off in (("mid", half_win), ("top", 0), ("bot", ATT_KROWS - ATT_ROWS)):
        for p in range(npair):
            if kind == "mid" and p > 0:
                continue
            rs_rel = []
            for a in range(ATT_PAIR):
                rq = ATT_PAIR * p + a
                if kind == "mid":
                    rs_rel.append(rq)
                elif kind == "top":
                    rs_rel.append(max(rq - half_win, 0))
                else:
                    rs_rel.append(min(rq + off - half_win, ATT_KROWS - WIN_H))
            m0 = min(rs_rel[0] // 2, (ATT_KROWS - ATT_BAND) // 2)
            sets.append((kind, p, off, rs_rel, m0))
    return sets


def _att_bias_table(rpb):
    n_heads, n_dr, n_dc = rpb.shape
    sets = _att_band_layout(None)
    cq = np.arange(GRID_W)
    cs = np.clip(cq - WIN_W // 2, 0, GRID_W - WIN_W)
    ck = np.arange(GRID_W)
    col_ok = (ck[None, :] >= cs[:, None]) & (ck[None, :] < cs[:, None] + WIN_W)
    dc = ck[None, :] - cq[:, None] + (WIN_W - 1)
    sel = ((dc[:, None, :] == np.arange(n_dc)[None, :, None]) & col_ok[:, None, :]).astype(np.float32)
    blocks = jnp.einsum("hrx,qxk->hrqk", rpb.astype(F32), jnp.asarray(sel), precision=lax.Precision.HIGHEST)
    blocks = jnp.where(jnp.asarray(col_ok)[None, None], blocks, NEG_BIG)
    blocks = jnp.concatenate([blocks, jnp.full((n_heads, 1, GRID_W, GRID_W), NEG_BIG, F32)], axis=1)
    which = np.full((len(sets), ATT_PAIR, ATT_BAND), n_dr, np.int32)
    for si, (kind, p, off, rs_rel, m0) in enumerate(sets):
        for a in range(ATT_PAIR):
            rq = ATT_PAIR * p + a if kind != "mid" else a
            rs = rs_rel[a] if kind != "mid" else a
            base = m0 if kind != "mid" else 0
            for j in range(ATT_BAND):
                rk = 2 * base + j
                if rs <= rk < rs + WIN_H:
                    which[si, a, j] = rk - rq - off + (WIN_H - 1)
    pairs = which.reshape(len(sets), ATT_PAIR, ATT_BAND // 2, 2)
    uniq, inv = np.unique(pairs.reshape(-1, 2), axis=0, return_inverse=True)
    lane_blocks = jnp.concatenate([jnp.take(blocks, jnp.asarray(uniq[:, 0]), axis=1),
                                   jnp.take(blocks, jnp.asarray(uniq[:, 1]), axis=1)], axis=-1)
    tab = jnp.take(lane_blocks, jnp.asarray(inv.reshape(-1)), axis=1)
    tab = tab.reshape(n_heads // 2, 2, len(sets), ATT_PAIR, ATT_BAND // 2, GRID_W, 2 * GRID_W)
    tab = tab.transpose(0, 2, 4, 1, 3, 5, 6)
    return tab.reshape(n_heads // 2, len(sets), ATT_BAND // 2, 2 * ATT_PAIR * GRID_W, 2 * GRID_W)


def _attn_kernel(q_ref, k0, k1, k2, k3, v0, v1, v2, v3, kc_ref, vc_ref, tb_ref, o_ref,
                 kwin, vwin, vcx, *, m0_top, m0_bot, bounded):
    i = pl.program_id(2)
    nt = pl.num_programs(2)
    ck = k0.shape[2]
    for c, (kr, vr) in enumerate(((k0, v0), (k1, v1), (k2, v2), (k3, v3))):
        kwin[c * ck:(c + 1) * ck, :] = kr[0, 0]
        vwin[c * ck:(c + 1) * ck, 0:LANES] = vr[0, 0]
    kc = kc_ref[0]
    if bounded:
        vwin[:, LANES:] = jnp.ones((vwin.shape[0], LANES), BF16)
        vcx[:, 0:LANES] = vc_ref[0]
        vcx[:, LANES:] = jnp.ones((vcx.shape[0], LANES), BF16)
        vc = vcx[...]
    else:
        vc = vc_ref[0]
    lane = lax.broadcasted_iota(jnp.int32, (1, LANES), 1)
    head_lo = lane < (LANES // 2)
    nq = ATT_PAIR * GRID_W
    nk = ATT_BAND * GRID_W
    npair = ATT_ROWS // ATT_PAIR
    dn = (((1,), (1,)), ((), ()))
    q2s, ss, vbs = [], [], []
    for p in range(npair):
        m0 = jnp.where(i == 0, m0_top[p], jnp.where(i == nt - 1, m0_bot[p], p))
        st = jnp.where(i == 0, 1 + p, jnp.where(i == nt - 1, 1 + npair + p, 0))
        start = pl.multiple_of(m0 * LANES, LANES)
        kb = kwin[pl.ds(start, nk), :]
        vbs.append(vwin[pl.ds(start, nk), :])
        qp = q_ref[0, p * nq:(p + 1) * nq, :]
        zero = jnp.zeros_like(qp)
        q2 = jnp.concatenate([jnp.where(head_lo, qp, zero), jnp.where(head_lo, zero, qp)], axis=0)
        q2s.append(q2)
        bias = jnp.concatenate([tb_ref[0, st, lb] for lb in range(tb_ref.shape[2])], axis=1)
        ss.append(lax.dot_general(q2, kb, dn, preferred_element_type=F32) + bias)
    sc_all = lax.dot_general(jnp.concatenate(q2s, axis=0), kc, dn, preferred_element_type=F32)
    for p in range(npair):
        s = ss[p]
        vb = vbs[p]
        sc = sc_all[p * 2 * nq:(p + 1) * 2 * nq]
        if bounded:
            ov = (jnp.dot(jnp.exp(s).astype(BF16), vb, preferred_element_type=F32)
                  + jnp.dot(jnp.exp(sc).astype(BF16), vc, preferred_element_type=F32))
            o2 = ov[:, 0:LANES] * (1.0 / ov[:, LANES:])
        else:
            m = jnp.maximum(jnp.max(s, axis=-1, keepdims=True), jnp.max(sc, axis=-1, keepdims=True))
            e = jnp.exp(s - m)
            ec = jnp.exp(sc - m)
            den = jnp.sum(e, axis=-1, keepdims=True) + jnp.sum(ec, axis=-1, keepdims=True)
            o2 = (jnp.dot(e.astype(BF16), vb, preferred_element_type=F32)
                  + jnp.dot(ec.astype(BF16), vc, preferred_element_type=F32)) * (1.0 / den)
        o_ref[0, p * nq:(p + 1) * nq, :] = jnp.where(head_lo, o2[:nq], o2[nq:]).astype(BF16)


def _attention(q, k, v, kc, vc, bias_tab, bounded):
    b, s, d = q.shape
    vlanes = 2 * LANES if bounded else LANES
    c = kc.shape[1]
    rows = s // GRID_W
    assert rows % ATT_ROWS == 0 and rows >= ATT_KROWS
    tq = ATT_ROWS * GRID_W
    nt = s // tq
    chunk_rows = 4
    ck = chunk_rows * GRID_W
    nchunk = ATT_KROWS // chunk_rows
    k4 = k.reshape(b, s // ck, ck, d)
    v4 = v.reshape(b, s // ck, ck, d)
    sets = _att_band_layout(None)
    npair = ATT_ROWS // ATT_PAIR
    m0_top = tuple(sets[1 + p][4] for p in range(npair))
    m0_bot = tuple(sets[1 + npair + p][4] for p in range(npair))

    def kmap(t):
        def f(bi, j, i):
            c0 = jnp.clip(i * (ATT_ROWS // chunk_rows) - (WIN_H // 2) // chunk_rows, 0,
                          (rows - ATT_KROWS) // chunk_rows)
            return (bi, c0 + t, 0, j)
        return f

    kv_specs = [pl.BlockSpec((1, 1, ck, LANES), kmap(t)) for t in range(nchunk)]
    return pl.pallas_call(
        functools.partial(_attn_kernel, m0_top=m0_top, m0_bot=m0_bot, bounded=bounded),
        grid=(b, d // LANES, nt),
        in_specs=[pl.BlockSpec((1, tq, LANES), lambda bi, j, i: (bi, i, j))]
                 + kv_specs + kv_specs
                 + [pl.BlockSpec((1, c, LANES), lambda bi, j, i: (bi, 0, j)),
                    pl.BlockSpec((1, c, LANES), lambda bi, j, i: (bi, 0, j)),
                    pl.BlockSpec((1,) + bias_tab.shape[1:], lambda bi, j, i: (j, 0, 0, 0, 0))],
        out_specs=pl.BlockSpec((1, tq, LANES), lambda bi, j, i: (bi, i, j)),
        out_shape=jax.ShapeDtypeStruct((b, s, d), BF16),
        scratch_shapes=[pltpu.VMEM((ATT_KROWS * GRID_W, LANES), BF16),
                        pltpu.VMEM((ATT_KROWS * GRID_W, vlanes), BF16),
                        pltpu.VMEM((c, vlanes), BF16)],
        compiler_params=_cparams(("parallel", "parallel", "arbitrary")),
        name="nbr_attention_bounded" if bounded else "nbr_attention",
    )(q, *([k4] * nchunk), *([v4] * nchunk), kc, vc, bias_tab)


def _ctx_attn_kernel(q_ref, k_ref, v_ref, o_ref):
    lane = lax.broadcasted_iota(jnp.int32, (1, LANES), 1)
    head_lo = lane < (LANES // 2)
    q = q_ref[0]
    k = k_ref[0]
    v = v_ref[0]
    outs = []
    for hh in range(2):
        mask = head_lo if hh == 0 else jnp.logical_not(head_lo)
        qh = jnp.where(mask, q, jnp.zeros_like(q))
        s = lax.dot_general(qh, k, (((1,), (1,)), ((), ())), preferred_element_type=F32)
        m = jnp.max(s, axis=-1, keepdims=True)
        e = jnp.exp(s - m)
        den = jnp.sum(e, axis=-1, keepdims=True)
        o = jnp.dot(e.astype(BF16), v, preferred_element_type=F32)
        outs.append(o * (1.0 / den))
    o_ref[0] = jnp.where(head_lo, outs[0], outs[1]).astype(BF16)


def _ctx_attention(q, k, v):
    b, c, d = q.shape
    spec = pl.BlockSpec((1, c, LANES), lambda bi, j: (bi, 0, j))
    return pl.pallas_call(
        _ctx_attn_kernel,
        grid=(b, d // LANES),
        in_specs=[spec, spec, spec],
        out_specs=spec,
        out_shape=jax.ShapeDtypeStruct((b, c, d), BF16),
        compiler_params=_cparams(("parallel", "parallel")),
        name="ctx_attention",
    )(q, k, v)


def _swiglu_chunks(h, w1_at, w3_at, w2_at, ff, tf):
    acc = None
    for c0 in range(0, ff, tf):
        a = jnp.dot(h, w1_at(c0, tf), preferred_element_type=F32)
        b = jnp.dot(h, w3_at(c0, tf), preferred_element_type=F32)
        t = (a * jax.nn.sigmoid(a) * b).astype(BF16)
        y = jnp.dot(t, w2_at(c0, tf), preferred_element_type=F32)
        acc = y if acc is None else acc + y
    return acc


def _attnproj_ffn_kernel(a_ref, wo_ref, x_ref, gate_a_ref, g_ref, sh_ref, sc_ref, gate_ref,
                         w1_ref, w3_ref, w2_ref, o_ref, *, tf):
    x1 = x_ref[0] + gate_a_ref[0] * jnp.dot(a_ref[0], wo_ref[...], preferred_element_type=F32)
    h = _rms_mod(x1, g_ref[...], sh_ref[0], sc_ref[0]).astype(BF16)
    y = _swiglu_chunks(h, lambda c0, n: w1_ref[:, c0:c0 + n], lambda c0, n: w3_ref[:, c0:c0 + n],
                       lambda c0, n: w2_ref[c0:c0 + n, :], w1_ref.shape[1], tf)
    o_ref[0] = x1 + gate_ref[0] * y


def _attnproj_ffn(a, wo_bf, x, gate_a, norm_g, shift, scale, gate, w1_bf, w3_bf, w2_bf, tm, tf):
    bx, sx, d = x.shape
    ff = w1_bf.shape[1]
    row = lambda b, i: (b, i, 0)
    vec = lambda b, i: (b, 0, 0)
    const2 = lambda b, i: (0, 0)
    once = pl.Buffered(1)
    return pl.pallas_call(
        functools.partial(_attnproj_ffn_kernel, tf=tf),
        grid=(bx, sx // tm),
        in_specs=[pl.BlockSpec((1, tm, d), row),
                  pl.BlockSpec((d, d), const2, pipeline_mode=once),
                  pl.BlockSpec((1, tm, d), row),
                  pl.BlockSpec((1, 1, d), vec),
                  pl.BlockSpec((1, d), const2),
                  pl.BlockSpec((1, 1, d), vec),
                  pl.BlockSpec((1, 1, d), vec),
                  pl.BlockSpec((1, 1, d), vec),
                  pl.BlockSpec((d, ff), const2, pipeline_mode=once),
                  pl.BlockSpec((d, ff), const2, pipeline_mode=once),
                  pl.BlockSpec((ff, d), const2, pipeline_mode=once)],
        out_specs=pl.BlockSpec((1, tm, d), row),
        out_shape=jax.ShapeDtypeStruct(x.shape, F32),
        compiler_params=_cparams(("parallel", "parallel")),
        name="attnproj_ffn",
    )(a, wo_bf, x, gate_a, norm_g.reshape(1, d), shift, scale, gate, w1_bf, w3_bf, w2_bf)


def _inproj_kernel(x_ref, xp_ref, xn_ref, g_ref, sh_ref, sc_ref, w_ref, cw_ref, cb_ref, gb_ref, xc_ref, ext,
                   *, dr):
    i = pl.program_id(1)
    nb = pl.num_programs(1)
    tm = x_ref.shape[1]
    pad = SUBLANES
    xe = jnp.concatenate([xp_ref[0], x_ref[0], xn_ref[0]], axis=0)
    h = _rms_mod(xe, g_ref[...], sh_ref[0], sc_ref[0]).astype(BF16)
    rowi = lax.broadcasted_iota(jnp.int32, (tm + 2 * pad, 1), 0)
    inside = jnp.logical_and(jnp.logical_or(rowi >= pad, i > 0),
                             jnp.logical_or(rowi < tm + pad, i < nb - 1))
    xr = jnp.dot(h, w_ref[:, dr:], preferred_element_type=F32)
    ext[...] = jnp.where(inside, xr, 0.0)
    cw = cw_ref[...]
    xc = cb_ref[...] + sum(
        cw[j:j + 1, :] * ext[pad + j - CONV_LEFT:pad + j - CONV_LEFT + tm, :] for j in range(cw.shape[0]))
    for k in range(dr // LANES):
        xc_ref[0, k] = xc[:, k * LANES:(k + 1) * LANES]
    gb_ref[0] = jnp.dot(h, w_ref[:, :dr], preferred_element_type=F32)[pad:pad + tm]


def _inproj(x, norm_g, shift, scale, w_in_bf, conv_w, conv_b, tm):
    bx, sx, d = x.shape
    dr = w_in_bf.shape[1] // 2
    tb = tm // SUBLANES
    row = lambda b, i: (b, i, 0)
    vec = lambda b, i: (b, 0, 0)
    const2 = lambda b, i: (0, 0)
    out = jax.ShapeDtypeStruct((bx, sx, dr), F32)
    return pl.pallas_call(
        functools.partial(_inproj_kernel, dr=dr),
        grid=(bx, sx // tm),
        in_specs=[pl.BlockSpec((1, tm, d), row),
                  pl.BlockSpec((1, SUBLANES, d), lambda b, i: (b, jnp.maximum(i * tb - 1, 0), 0)),
                  pl.BlockSpec((1, SUBLANES, d),
                               lambda b, i: (b, jnp.minimum((i + 1) * tb, sx // SUBLANES - 1), 0)),
                  pl.BlockSpec((1, d), const2),
                  pl.BlockSpec((1, 1, d), vec),
                  pl.BlockSpec((1, 1, d), vec),
                  pl.BlockSpec((d, 2 * dr), const2),
                  pl.BlockSpec(conv_w.shape, const2),
                  pl.BlockSpec((1, dr), const2)],
        out_specs=[pl.BlockSpec((1, tm, dr), row),
                   pl.BlockSpec((1, dr // LANES, tm, LANES), lambda b, i: (b, 0, i, 0))],
        out_shape=[out, jax.ShapeDtypeStruct((bx, dr // LANES, sx, LANES), F32)],
        scratch_shapes=[pltpu.VMEM((tm + 2 * SUBLANES, dr), F32)],
        compiler_params=_cparams(("parallel", "parallel")),
        name="rglru_inproj",
    )(x, x, x, norm_g.reshape(1, d), shift, scale, w_in_bf, conv_w, conv_b.reshape(1, dr))


def _lru_scan_kernel(xc_hbm, wa_ref, wx_ref, ba_ref, bx_ref, lam_ref, h0_ref, o_hbm, xbuf, obuf, carry,
                     sem_in, sem_out, *, reverse, ngroups):
    bi = pl.program_id(0)
    i = pl.program_id(1)
    nb = pl.num_programs(1)
    ncg, nl = xbuf.shape[1], xbuf.shape[2]
    t = nl * SUBLANES
    c = ncg * LANES
    slot = i % 2

    def block_start(step):
        return ((nb - 1 - step) if reverse else step) * t

    def in_copies(step, to_slot):
        return [pltpu.make_async_copy(xc_hbm.at[bi, :, pl.ds(block_start(step) + ch * nl, nl), :],
                                      xbuf.at[to_slot, :, :, ch, :], sem_in.at[to_slot])
                for ch in range(SUBLANES)]

    def out_copies(step, from_slot):
        return [pltpu.make_async_copy(obuf.at[from_slot, :, :, ch, :],
                                      o_hbm.at[bi, :, pl.ds(block_start(step) + ch * nl, nl), :],
                                      sem_out.at[from_slot])
                for ch in range(SUBLANES)]

    @pl.when(i == 0)
    def _():
        carry[...] = jnp.broadcast_to(h0_ref[0], carry.shape)
        for cp in in_copies(0, 0):
            cp.start()

    @pl.when(i + 1 < nb)
    def _():
        for cp in in_copies(i + 1, 1 - slot):
            cp.start()

    for cp in in_copies(i, slot):
        cp.wait()

    @pl.when(i >= 2)
    def _():
        for cp in out_copies(i, slot):
            cp.wait()

    gw = c // ngroups
    kpg = gw // LANES
    steps = range(nl - 1, -1, -1) if reverse else range(nl)
    rows = lambda v, j: v[j * SUBLANES:(j + 1) * SUBLANES]
    sub = lax.broadcasted_iota(jnp.int32, (SUBLANES, 1), 0)
    for gi in range(ngroups):
        cs = slice(gi * gw, (gi + 1) * gw)
        xc = jnp.concatenate([xbuf[slot, gi * kpg + k].reshape(t, LANES) for k in range(kpg)], axis=1)
        xb = xc.astype(BF16)
        t_a = jnp.tanh(jnp.dot(xb, wa_ref[gi], preferred_element_type=F32) + 0.5 * ba_ref[:, cs])
        t_x = jnp.tanh(jnp.dot(xb, wx_ref[gi], preferred_element_type=F32) + 0.5 * bx_ref[:, cs])
        lam = lam_ref[:, cs]
        sp = jnp.maximum(-lam, 0.0) + jnp.log(1.0 + jnp.exp(-jnp.abs(lam)))
        k1 = ((-0.5 * LRU_C) * np.float32(np.log2(np.e))) * sp
        a = jnp.exp2(t_a * k1 + k1)
        om = 1.0 - a * a
        u = (om * lax.rsqrt(jnp.maximum(om, 1e-37))) * ((0.5 * t_x + 0.5) * xc)

        hloc = jnp.zeros((SUBLANES, gw), F32)
        aprod = jnp.ones((SUBLANES, gw), F32)
        for j in steps:
            hloc = rows(a, j) * hloc + rows(u, j)
            aprod = rows(a, j) * aprod

        hprev = carry[:, cs]
        for sft in (1, 2, 4):
            if reverse:
                a_sh = pltpu.roll(aprod, SUBLANES - sft, axis=0)
                h_sh = pltpu.roll(hloc, SUBLANES - sft, axis=0)
                ok = sub < SUBLANES - sft
            else:
                a_sh = pltpu.roll(aprod, sft, axis=0)
                h_sh = pltpu.roll(hloc, sft, axis=0)
                ok = sub >= sft
            hloc = jnp.where(ok, aprod * h_sh + hloc, hloc)
            aprod = jnp.where(ok, aprod * a_sh, aprod)
        after = hloc + aprod * hprev
        if reverse:
            h = jnp.where(sub == SUBLANES - 1, hprev, pltpu.roll(after, SUBLANES - 1, axis=0))
            carry[:, cs] = jnp.broadcast_to(after[0:1], after.shape)
        else:
            h = jnp.where(sub == 0, hprev, pltpu.roll(after, 1, axis=0))
            carry[:, cs] = jnp.broadcast_to(after[SUBLANES - 1:SUBLANES], after.shape)

        for j in steps:
            h = rows(a, j) * h + rows(u, j)
            for k in range(kpg):
                obuf[slot, gi * kpg + k, j] = h[:, k * LANES:(k + 1) * LANES]
    for cp in out_copies(i, slot):
        cp.start()

    @pl.when(i == nb - 1)
    def _():
        for cp in out_copies(i, slot):
            cp.wait()

    @pl.when(jnp.logical_and(i == nb - 1, nb >= 2))
    def _():
        for cp in out_copies(i, 1 - slot):
            cp.wait()


def _lru_scan(xc, wa_g, wx_g, ba, bx, lam, h0, reverse, t):
    b, ncg, s, _ = xc.shape
    c = ncg * LANES
    nb = s // t
    nl = t // SUBLANES
    ngroups = wa_g.shape[0]
    gw = c // ngroups
    const2 = lambda bi, i: (0, 0)
    const3 = lambda bi, i: (0, 0, 0)
    return pl.pallas_call(
        functools.partial(_lru_scan_kernel, reverse=reverse, ngroups=ngroups),
        grid=(b, nb),
        in_specs=[pl.BlockSpec(memory_space=pl.ANY),
                  pl.BlockSpec((ngroups, gw, gw), const3),
                  pl.BlockSpec((ngroups, gw, gw), const3),
                  pl.BlockSpec((1, c), const2),
                  pl.BlockSpec((1, c), const2),
                  pl.BlockSpec((1, c), const2),
                  pl.BlockSpec((1, 1, c), lambda bi, i: (bi, 0, 0))],
        out_specs=pl.BlockSpec(memory_space=pl.ANY),
        out_shape=jax.ShapeDtypeStruct((b, ncg, s, LANES), F32),
        scratch_shapes=[pltpu.VMEM((2, ncg, nl, SUBLANES, LANES), F32),
                        pltpu.VMEM((2, ncg, nl, SUBLANES, LANES), F32),
                        pltpu.VMEM((SUBLANES, c), F32),
                        pltpu.SemaphoreType.DMA((2,)),
                        pltpu.SemaphoreType.DMA((2,))],
        compiler_params=_cparams(("parallel", "arbitrary")),
        name="lru_scan_bwd" if reverse else "lru_scan_fwd",
    )(xc, wa_g, wx_g, ba.reshape(1, c), bx.reshape(1, c), lam.reshape(1, c), h0)


def _lru_out_router_kernel(g_ref, hf_ref, hb_ref, w_ref, x_ref, gate_ref, ng_ref, sh_ref, sc_ref,
                           wh_ref, wl_ref, br_ref, o_ref, hp_ref, rt_ref, *, n_exp):
    hs = jnp.concatenate([hf_ref[0, k] + hb_ref[0, k] for k in range(hf_ref.shape[1])], axis=1)
    a = (jax.nn.gelu(g_ref[0]) * hs).astype(BF16)
    x1 = x_ref[0] + gate_ref[0] * jnp.dot(a, w_ref[...], preferred_element_type=F32)
    o_ref[0] = x1
    h = _rms_mod(x1, ng_ref[...], sh_ref[0], sc_ref[0])
    hb = h.astype(BF16)
    hf = hb.astype(F32)
    h_lo = (h - hf).astype(BF16)
    logits = (jnp.dot(hb, wh_ref[...], preferred_element_type=F32)
              + jnp.dot(hb, wl_ref[...], preferred_element_type=F32)
              + jnp.dot(h_lo, wh_ref[...], preferred_element_type=F32)) + br_ref[...]
    lane = lax.broadcasted_iota(jnp.int32, logits.shape, 1)
    logits = jnp.where(lane < n_exp, logits, -jnp.inf)
    m1 = jnp.max(logits, axis=-1, keepdims=True)
    i1 = jnp.min(jnp.where(logits == m1, lane, LANES), axis=-1, keepdims=True)
    rest = jnp.where(lane == i1, -jnp.inf, logits)
    m2 = jnp.max(rest, axis=-1, keepdims=True)
    i2 = jnp.min(jnp.where(rest == m2, lane, LANES), axis=-1, keepdims=True)
    e2 = jnp.exp(m2 - m1)
    g1 = 1.0 / (1.0 + e2)
    g2 = e2 * g1
    rt_ref[0] = jnp.where(lane == 0, g1, jnp.where(lane == 1, g2, jnp.where(
        lane == 2, i1.astype(F32), jnp.where(lane == 3, i2.astype(F32), 0.0))))
    hp_ref[0] = h


def _lru_out_router(g, hf, hb, w_out_bf, x, gate, norm_g, shift, scale, w_router, b_router, tm):
    bx, sx, dr = g.shape
    d = x.shape[-1]
    n_exp = w_router.shape[1]
    wpad = jnp.zeros((d, LANES), F32).at[:, :n_exp].set(w_router)
    w_hi = wpad.astype(BF16)
    w_lo = (wpad - w_hi.astype(F32)).astype(BF16)
    bpad = jnp.zeros((1, LANES), F32).at[0, :n_exp].set(b_router)
    row = lambda b, i: (b, i, 0)
    vec = lambda b, i: (b, 0, 0)
    const2 = lambda b, i: (0, 0)
    cg = pl.BlockSpec((1, dr // LANES, tm, LANES), lambda b, i: (b, 0, i, 0))
    return pl.pallas_call(
        functools.partial(_lru_out_router_kernel, n_exp=n_exp),
        grid=(bx, sx // tm),
        in_specs=[pl.BlockSpec((1, tm, dr), row), cg, cg,
                  pl.BlockSpec((dr, d), const2),
                  pl.BlockSpec((1, tm, d), row),
                  pl.BlockSpec((1, 1, d), vec),
                  pl.BlockSpec((1, d), const2),
                  pl.BlockSpec((1, 1, d), vec),
                  pl.BlockSpec((1, 1, d), vec),
                  pl.BlockSpec((d, LANES), const2),
                  pl.BlockSpec((d, LANES), const2),
                  pl.BlockSpec((1, LANES), const2)],
        out_specs=[pl.BlockSpec((1, tm, d), row), pl.BlockSpec((1, tm, d), row),
                   pl.BlockSpec((1, tm, LANES), row)],
        out_shape=[jax.ShapeDtypeStruct(x.shape, F32), jax.ShapeDtypeStruct(x.shape, F32),
                   jax.ShapeDtypeStruct((bx, sx, LANES), F32)],
        compiler_params=_cparams(("parallel", "parallel")),
        name="rglru_outproj_router",
    )(g, hf, hb, w_out_bf, x, gate, norm_g.reshape(1, d), shift, scale, w_hi, w_lo, bpad)


def _experts_kernel(te_ref, nu_ref, dst_ref, h_hbm, w1_ref, w3_ref, w2_ref, y_hbm,
                    xbuf_a, xbuf_b, ybuf_a, ybuf_b, gsem, ssem, *, tf, n_tok):
    del te_ref
    i = pl.program_id(0)
    nu = nu_ref[0]
    tm = xbuf_a.shape[0]
    xbufs = (xbuf_a, xbuf_b)
    ybufs = (ybuf_a, ybuf_b)

    def gather_copies(tile, to_slot):
        base = tile * tm
        return [pltpu.make_async_copy(
            h_hbm.at[pl.ds(jnp.minimum(dst_ref[base + r] >> 1, n_tok - 1), 1)],
            xbufs[to_slot].at[pl.ds(r, 1)], gsem.at[to_slot]) for r in range(tm)]

    def scatter_copies(tile, from_slot):
        base = tile * tm
        return [pltpu.make_async_copy(ybufs[from_slot].at[pl.ds(r, 1)], y_hbm.at[pl.ds(dst_ref[base + r], 1)],
                                      ssem.at[from_slot]) for r in range(tm)]

    def wait_gather(of_slot):
        pltpu.make_async_copy(h_hbm.at[pl.ds(0, tm)], xbufs[of_slot], gsem.at[of_slot]).wait()

    def wait_scatter(of_slot):
        pltpu.make_async_copy(ybufs[of_slot], y_hbm.at[pl.ds(0, tm)], ssem.at[of_slot]).wait()

    @pl.when(i == 0)
    def _():
        for cp in gather_copies(0, 0):
            cp.start()

    def tile_step(slot, has_prev):
        wait_gather(slot)

        @pl.when(i >= 2)
        def _():
            wait_scatter(slot)

        for cp in gather_copies(jnp.minimum(i + 1, nu - 1), 1 - slot):
            cp.start()
        if has_prev:
            for cp in scatter_copies(i - 1, 1 - slot):
                cp.start()
        ybufs[slot][...] = _swiglu_chunks(
            xbufs[slot][...].astype(BF16), lambda c0, n: w1_ref[0, :, c0:c0 + n],
            lambda c0, n: w3_ref[0, :, c0:c0 + n], lambda c0, n: w2_ref[0, c0:c0 + n, :],
            w1_ref.shape[2], tf)

        @pl.when(i == nu - 1)
        def _():
            wait_gather(1 - slot)
            if has_prev:
                wait_scatter(1 - slot)
            for cp in scatter_copies(i, slot):
                cp.start()
            wait_scatter(slot)

    pl.when(i == 0)(functools.partial(tile_step, 0, False))
    for par in range(2):
        pl.when(jnp.logical_and(jnp.logical_and(i >= 1, i < nu), i % 2 == par))(
            functools.partial(tile_step, par, True))

    @pl.when(i >= nu)
    def _():
        ybuf_a[...] = jnp.zeros_like(ybuf_a)
        for cp in scatter_copies(i, 0):
            cp.start()
        wait_scatter(0)


def _experts(h, row_dst, tile_expert, n_used, w1_bf, w3_bf, w2_bf, tm, tf):
    n_rows = row_dst.shape[0]
    n_tok = h.shape[0]
    n_exp, d, ff = w1_bf.shape
    n_tiles = n_rows // tm

    def live(i, nu):
        return jnp.minimum(i, nu[0] - 1)

    once = pl.Buffered(1)
    wmap = lambda i, te, nu, dst: (te[live(i, nu)], 0, 0)
    row_buf = pltpu.VMEM((tm, d), F32)
    return pl.pallas_call(
        functools.partial(_experts_kernel, tf=tf, n_tok=n_tok),
        grid_spec=pltpu.PrefetchScalarGridSpec(
            num_scalar_prefetch=3,
            grid=(n_tiles,),
            in_specs=[pl.BlockSpec(memory_space=pl.ANY),
                      pl.BlockSpec((1, d, ff), wmap, pipeline_mode=once),
                      pl.BlockSpec((1, d, ff), wmap, pipeline_mode=once),
                      pl.BlockSpec((1, ff, d), wmap, pipeline_mode=once)],
            out_specs=pl.BlockSpec(memory_space=pl.ANY),
            scratch_shapes=[row_buf, row_buf, row_buf, row_buf,
                            pltpu.SemaphoreType.DMA((2,)), pltpu.SemaphoreType.DMA((2,))]),
        out_shape=jax.ShapeDtypeStruct((n_rows, d), F32),
        compiler_params=pltpu.CompilerParams(dimension_semantics=("arbitrary",),
                                             vmem_limit_bytes=VMEM_LIMIT, has_side_effects=True),
        name="moe_experts",
    )(tile_expert, n_used, row_dst, h, w1_bf, w3_bf, w2_bf)


def _combine_kernel(x_ref, gate_ref, rt_ref, y_ref, o_ref):
    d = x_ref.shape[1]
    rt = rt_ref[...]
    y = rt[:, 0:1] * y_ref[:, 0:d] + rt[:, 1:2] * y_ref[:, d:2 * d]
    o_ref[...] = x_ref[...] + gate_ref[0] * y


def _combine(x2, gate, route, y_pairs, seq, tc):
    n, d = x2.shape
    return pl.pallas_call(
        _combine_kernel,
        grid=(n // tc,),
        in_specs=[pl.BlockSpec((tc, d), lambda i: (i, 0)),
                  pl.BlockSpec((1, 1, d), lambda i: ((i * tc) // seq, 0, 0)),
                  pl.BlockSpec((tc, LANES), lambda i: (i, 0)),
                  pl.BlockSpec((tc, TOP_K * d), lambda i: (i, 0))],
        out_specs=pl.BlockSpec((tc, d), lambda i: (i, 0)),
        out_shape=jax.ShapeDtypeStruct((n, d), F32),
        compiler_params=_cparams(("parallel",)),
        name="moe_combine",
    )(x2, gate, route, y_pairs)


def _route_positions(e_flat, n_exp, tm):
    a = e_flat.shape[0]
    onehot = (e_flat[:, None] == jnp.arange(n_exp, dtype=jnp.int32)[None, :]).astype(jnp.int32)
    csum = jnp.cumsum(onehot, axis=0)
    counts = csum[-1]
    padded = ((counts + tm - 1) // tm) * tm
    ends = jnp.cumsum(padded)
    starts = ends - padded
    dest = jnp.sum(onehot * (csum - 1 + starts[None, :]), axis=1).astype(jnp.int32)
    n_tiles = a // tm + n_exp
    tile_ids = jnp.arange(n_tiles, dtype=jnp.int32)
    tile_expert = jnp.minimum(jnp.sum((ends[None, :] // tm <= tile_ids[:, None]).astype(jnp.int32), axis=1),
                              n_exp - 1)
    n_used = (ends[-1] // tm).astype(jnp.int32).reshape(1)
    n_slots = n_tiles * tm
    routed = jnp.zeros((n_slots,), jnp.int32).at[dest].set(1, unique_indices=True)
    assign = jnp.zeros((n_slots,), jnp.int32).at[dest].set(jnp.arange(a, dtype=jnp.int32), unique_indices=True)
    free_rank = jnp.cumsum(1 - routed) - 1
    row_dst = jnp.where(routed == 1, assign, a + free_rank).astype(jnp.int32)
    return row_dst, tile_expert.astype(jnp.int32), n_used


def _blockdiag_groups(w, ngroups):
    nblk, bw, _ = w.shape
    per = nblk // ngroups
    eye = jnp.eye(per, dtype=w.dtype)
    wg = w.reshape(ngroups, per, bw, bw)
    full = wg[:, :, :, None, :] * eye[None, :, None, :, None]
    return full.reshape(ngroups, per * bw, per * bw)


def kernel(x, c, ctx, c_ctx, l0_w_mod, l0_b_mod, l0_norm1, l0_norm2, l0_w_qkv, l0_q_gain, l0_k_gain, l0_rpb, l0_w_o, l0_ffn_w1, l0_ffn_w3, l0_ffn_w2, l1_w_mod, l1_b_mod, l1_norm1, l1_norm2, l1_w_in, l1_conv_w, l1_conv_b, l1_gate_a_w, l1_gate_a_b, l1_gate_x_w, l1_gate_x_b, l1_lam, l1_w_out, l1_router_w, l1_router_b, l1_moe_w1, l1_moe_w3, l1_moe_w2):
    b, s, d = x.shape
    n_ctx = ctx.shape[1]
    head_dim = l0_q_gain.shape[0]
    n_heads = d // head_dim
    n_exp = l1_router_w.shape[1]
    tm = min(512, s)
    tmc = min(512, n_ctx)

    def mods(w_mod, b_mod):
        m = _adaln_mod(jnp.concatenate([c, c_ctx[None, :]], axis=0), w_mod, b_mod)
        lat = [m[:b, j * d:(j + 1) * d].reshape(b, 1, d) for j in range(6)]
        cx = [jnp.broadcast_to(m[b:b + 1, j * d:(j + 1) * d].reshape(1, 1, d), (b, 1, d)) for j in range(6)]
        return lat, cx

    ml, mc = mods(l0_w_mod, l0_b_mod)
    w_qkv = l0_w_qkv.astype(BF16)
    qg = (jnp.tile(l0_q_gain, n_heads) * (head_dim ** -0.5)).reshape(1, d)
    kg = jnp.tile(l0_k_gain, n_heads).reshape(1, d)
    ql, kl, vl = _qkv(x, l0_norm1, ml[0], ml[1], w_qkv, qg, kg, head_dim, tm)
    qc, kc, vc = _qkv(ctx, l0_norm1, mc[0], mc[1], w_qkv, qg, kg, head_dim, tmc)
    logit_bound = (1.02 * head_dim) * jnp.max(jnp.abs(qg)) * jnp.max(jnp.abs(kg)) + jnp.max(jnp.abs(l0_rpb))
    bias_tab = _att_bias_table(l0_rpb)
    o_l = lax.cond(logit_bound <= ATT_LOGIT_BOUND,
                   lambda: _attention(ql, kl, vl, kc, vc, bias_tab, True),
                   lambda: _attention(ql, kl, vl, kc, vc, bias_tab, False))
    o_c = _ctx_attention(qc, kc, vc)
    w_o = l0_w_o.astype(BF16)
    w1, w3, w2 = l0_ffn_w1.astype(BF16), l0_ffn_w3.astype(BF16), l0_ffn_w2.astype(BF16)
    tf0 = l0_ffn_w1.shape[1]
    x_l = _attnproj_ffn(o_l, w_o, x, ml[2], l0_norm2, ml[3], ml[4], ml[5], w1, w3, w2, tm, tf0)
    x_c = _attnproj_ffn(o_c, w_o, ctx, mc[2], l0_norm2, mc[3], mc[4], mc[5], w1, w3, w2, tmc, tf0)

    ml, mc = mods(l1_w_mod, l1_b_mod)
    w_in = l1_w_in.astype(BF16)
    g_l, xr_l = _inproj(x_l, l1_norm1, ml[0], ml[1], w_in, l1_conv_w, l1_conv_b, tm)
    _, xr_c = _inproj(x_c, l1_norm1, mc[0], mc[1], w_in, l1_conv_w, l1_conv_b, tmc)
    dr = g_l.shape[-1]
    ngroups = 4
    t_scan = min(256, n_ctx)
    zeros0 = jnp.zeros((b, 1, dr), F32)
    hs = []
    for di, rev in enumerate((False, True)):
        wa_g = (0.5 * _blockdiag_groups(l1_gate_a_w[di], ngroups)).astype(BF16)
        wx_g = (0.5 * _blockdiag_groups(l1_gate_x_w[di], ngroups)).astype(BF16)
        args = (wa_g, wx_g, l1_gate_a_b[di], l1_gate_x_b[di], l1_lam[di])
        h_c = _lru_scan(xr_c, *args, zeros0, rev, t_scan)
        h0 = (h_c[:, :, 0, :] if rev else h_c[:, :, n_ctx - 1, :]).reshape(b, 1, dr)
        hs.append(_lru_scan(xr_l, *args, h0, rev, t_scan))
    x_l, hp, route = _lru_out_router(g_l, hs[0], hs[1], l1_w_out.astype(BF16), x_l, ml[2],
                                     l1_norm2, ml[3], ml[4], l1_router_w, l1_router_b, tm)
    n = b * s
    route2 = route.reshape(n, LANES)
    e_flat = route2[:, 2:2 + TOP_K].astype(jnp.int32).reshape(n * TOP_K)
    tme = 512
    row_dst, tile_expert, n_used = _route_positions(e_flat, n_exp, tme)
    y = _experts(hp.reshape(n, d), row_dst, tile_expert, n_used, l1_moe_w1.astype(BF16),
                 l1_moe_w3.astype(BF16), l1_moe_w2.astype(BF16), tme, l1_moe_w1.shape[2] // 2)
    out = _combine(x_l.reshape(n, d), ml[5], route2, y.reshape(-1, TOP_K * d), s, min(256, n))
    return out.reshape(b, s, d)
```

```python
import functools

import numpy as np
import jax
import jax.numpy as jnp
from jax import lax
from jax.experimental import pallas as pl
from jax.experimental.pallas import tpu as pltpu

GRID_W = 64
WIN_H = 8
WIN_W = 16
TOP_K = 2
LRU_C = 8.0
EPS = 1e-6
CONV_LEFT = 2
NEG_BIG = -1e30
LANES = 128
SUBLANES = 8
VMEM_LIMIT = 56 * 1024 * 1024

ATT_ROWS = 8
ATT_KROWS = 16
ATT_PAIR = 2
ATT_BAND = 10
ATT_LOGIT_BOUND = 40.0

F32 = jnp.float32
BF16 = jnp.bfloat16


def _cparams(sem):
    return pltpu.CompilerParams(dimension_semantics=sem, vmem_limit_bytes=VMEM_LIMIT)


def _rms_mod(x, g, shift, scale):
    ms = jnp.mean(x * x, axis=-1, keepdims=True)
    y = x * lax.rsqrt(ms + EPS) * g
    return y * (1.0 + scale) + shift


def _mod_kernel(ct_ref, w_ref, b_ref, o_ref, *, nrows):
    ct = ct_ref[...]
    s = ct * jax.nn.sigmoid(ct)
    w = w_ref[...]
    rows = [jnp.sum(w * s[:, m:m + 1], axis=0, keepdims=True) + b_ref[...] for m in range(nrows)]
    rows += [jnp.zeros_like(rows[0])] * (SUBLANES - nrows)
    o_ref[...] = jnp.concatenate(rows, axis=0)


def _adaln_mod(cvecs, w_mod, b_mod):
    nrows, d = cvecs.shape
    n = w_mod.shape[1]
    tn = 768
    ct = jnp.zeros((d, SUBLANES), F32).at[:, :nrows].set(cvecs.T)
    return pl.pallas_call(
        functools.partial(_mod_kernel, nrows=nrows),
        grid=(n // tn,),
        in_specs=[pl.BlockSpec((d, SUBLANES), lambda j: (0, 0)),
                  pl.BlockSpec((d, tn), lambda j: (0, j)),
                  pl.BlockSpec((1, tn), lambda j: (0, j))],
        out_specs=pl.BlockSpec((SUBLANES, tn), lambda j: (0, j)),
        out_shape=jax.ShapeDtypeStruct((SUBLANES, n), F32),
        compiler_params=_cparams(("arbitrary",)),
        name="adaln_mod",
    )(ct, w_mod, b_mod.reshape(1, n))


def _qkv_kernel(x_ref, g_ref, sh_ref, sc_ref, w_ref, qg_ref, kg_ref, pm_ref, pe_ref,
                q_ref, k_ref, v_ref, *, d):
    h = _rms_mod(x_ref[0], g_ref[...], sh_ref[0], sc_ref[0]).astype(BF16)
    for part, (gain_ref, out_ref) in enumerate(((qg_ref, q_ref), (kg_ref, k_ref))):
        y = jnp.dot(h, w_ref[:, part * d:(part + 1) * d], preferred_element_type=F32)
        ms = jnp.dot((y * y).astype(BF16), pm_ref[...], preferred_element_type=F32)
        inv = lax.rsqrt(ms + EPS)
        inv_hi = inv.astype(BF16)
        inv_lo = (inv - inv_hi.astype(F32)).astype(BF16)
        full = (jnp.dot(inv_hi, pe_ref[...], preferred_element_type=F32)
                + jnp.dot(inv_lo, pe_ref[...], preferred_element_type=F32))
        out_ref[0] = (y * full * gain_ref[...]).astype(BF16)
    v_ref[0] = jnp.dot(h, w_ref[:, 2 * d:3 * d], preferred_element_type=F32).astype(BF16)


def _qkv(x, norm_g, shift, scale, w_qkv_bf, q_gain_full, k_gain_full, head_dim, tm):
    bx, sx, d = x.shape
    n_heads = d // head_dim
    head_of = np.arange(d) // head_dim
    pm = np.zeros((d, LANES), np.float32)
    pm[np.arange(d), head_of] = 1.0 / head_dim
    pe = np.zeros((LANES, d), np.float32)
    pe[head_of, np.arange(d)] = 1.0
    assert n_heads <= LANES
    row = lambda b, i: (b, i, 0)
    vec = lambda b, i: (b, 0, 0)
    const2 = lambda b, i: (0, 0)
    out = jax.ShapeDtypeStruct((bx, sx, d), BF16)
    return pl.pallas_call(
        functools.partial(_qkv_kernel, d=d),
        grid=(bx, sx // tm),
        in_specs=[pl.BlockSpec((1, tm, d), row),
                  pl.BlockSpec((1, d), const2),
                  pl.BlockSpec((1, 1, d), vec),
                  pl.BlockSpec((1, 1, d), vec),
                  pl.BlockSpec((d, 3 * d), const2),
                  pl.BlockSpec((1, d), const2),
                  pl.BlockSpec((1, d), const2),
                  pl.BlockSpec((d, LANES), const2),
                  pl.BlockSpec((LANES, d), const2)],
        out_specs=[pl.BlockSpec((1, tm, d), row)] * 3,
        out_shape=[out, out, out],
        compiler_params=_cparams(("parallel", "parallel")),
        name="qkv_headnorm",
    )(x, norm_g.reshape(1, d), shift, scale, w_qkv_bf, q_gain_full, k_gain_full,
      jnp.asarray(pm, BF16), jnp.asarray(pe, BF16))


def _att_band_layout(rows):
    npair = ATT_ROWS // ATT_PAIR
    half_win = WIN_H // 2
    sets = []
    for kind, off in (("mid", half_win), ("top", 0), ("bot", ATT_KROWS - ATT_ROWS)):
        for p in range(npair):
            if kind == "mid" and p > 0:
                continue
            rs_rel = []
            for a in range(ATT_PAIR):
                rq = ATT_PAIR * p + a
                if kind == "mid":
                    rs_rel.append(rq)
                elif kind == "top":
                    rs_rel.append(max(rq - half_win, 0))
                else:
                    rs_rel.append(min(rq + off - half_win, ATT_KROWS - WIN_H))
            m0 = min(rs_rel[0] // 2, (ATT_KROWS - ATT_BAND) // 2)
            sets.append((kind, p, off, rs_rel, m0))
    return sets


def _att_bias_table(rpb):
    n_heads, n_dr, n_dc = rpb.shape
    sets = _att_band_layout(None)
    cq = np.arange(GRID_W)
    cs = np.clip(cq - WIN_W // 2, 0, GRID_W - WIN_W)
    ck = np.arange(GRID_W)
    col_ok = (ck[None, :] >= cs[:, None]) & (ck[None, :] < cs[:, None] + WIN_W)
    dc = ck[None, :] - cq[:, None] + (WIN_W - 1)
    sel = ((dc[:, None, :] == np.arange(n_dc)[None, :, None]) & col_ok[:, None, :]).astype(np.float32)
    blocks = jnp.einsum("hrx,qxk->hrqk", rpb.astype(F32), jnp.asarray(sel), precision=lax.Precision.HIGHEST)
    blocks = jnp.where(jnp.asarray(col_ok)[None, None], blocks, NEG_BIG)
    blocks = jnp.concatenate([blocks, jnp.full((n_heads, 1, GRID_W, GRID_W), NEG_BIG, F32)], axis=1)
    which = np.full((len(sets), ATT_PAIR, ATT_BAND), n_dr, np.int32)
    for si, (kind, p, off, rs_rel, m0) in enumerate(sets):
        for a in range(ATT_PAIR):
            rq = ATT_PAIR * p + a if kind != "mid" else a
            rs = rs_rel[a] if kind != "mid" else a
            base = m0 if kind != "mid" else 0
            for j in range(ATT_BAND):
                rk = 2 * base + j
                if rs <= rk < rs + WIN_H:
                    which[si, a, j] = rk - rq - off + (WIN_H - 1)
    pairs = which.reshape(len(sets), ATT_PAIR, ATT_BAND // 2, 2)
    uniq, inv = np.unique(pairs.reshape(-1, 2), axis=0, return_inverse=True)
    lane_blocks = jnp.concatenate([jnp.take(blocks, jnp.asarray(uniq[:, 0]), axis=1),
                                   jnp.take(blocks, jnp.asarray(uniq[:, 1]), axis=1)], axis=-1)
    tab = jnp.take(lane_blocks, jnp.asarray(inv.reshape(-1)), axis=1)
    tab = tab.reshape(n_heads // 2, 2, len(sets), ATT_PAIR, ATT_BAND // 2, GRID_W, 2 * GRID_W)
    tab = tab.transpose(0, 2, 4, 1, 3, 5, 6)
    return tab.reshape(n_heads // 2, len(sets), ATT_BAND // 2, 2 * ATT_PAIR * GRID_W, 2 * GRID_W)


def _attn_kernel(q_ref, k0, k1, k2, k3, v0, v1, v2, v3, kc_ref, vc_ref, tb_ref, o_ref,
                 kwin, vwin, vcx, *, m0_top, m0_bot, bounded):
    i = pl.program_id(2)
    nt = pl.num_programs(2)
    ck = k0.shape[2]
    for c, (kr, vr) in enumerate(((k0, v0), (k1, v1), (k2, v2), (k3, v3))):
        kwin[c * ck:(c + 1) * ck, :] = kr[0, 0]
        vwin[c * ck:(c + 1) * ck, 0:LANES] = vr[0, 0]
    kc = kc_ref[0]
    if bounded:
        vwin[:, LANES:] = jnp.ones((vwin.shape[0], LANES), BF16)
        vcx[:, 0:LANES] = vc_ref[0]
        vcx[:, LANES:] = jnp.ones((vcx.shape[0], LANES), BF16)
        vc = vcx[...]
    else:
        vc = vc_ref[0]
    lane = lax.broadcasted_iota(jnp.int32, (1, LANES), 1)
    head_lo = lane < (LANES // 2)
    nq = ATT_PAIR * GRID_W
    nk = ATT_BAND * GRID_W
    npair = ATT_ROWS // ATT_PAIR
    dn = (((1,), (1,)), ((), ()))
    q2s, ss, vbs = [], [], []
    for p in range(npair):
        m0 = jnp.where(i == 0, m0_top[p], jnp.where(i == nt - 1, m0_bot[p], p))
        st = jnp.where(i == 0, 1 + p, jnp.where(i == nt - 1, 1 + npair + p, 0))
        start = pl.multiple_of(m0 * LANES, LANES)
        kb = kwin[pl.ds(start, nk), :]
        vbs.append(vwin[pl.ds(start, nk), :])
        qp = q_ref[0, p * nq:(p + 1) * nq, :]
        zero = jnp.zeros_like(qp)
        q2 = jnp.concatenate([jnp.where(head_lo, qp, zero), jnp.where(head_lo, zero, qp)], axis=0)
        q2s.append(q2)
        bias = jnp.concatenate([tb_ref[0, st, lb] for lb in range(tb_ref.shape[2])], axis=1)
        ss.append(lax.dot_general(q2, kb, dn, preferred_element_type=F32) + bias)
    sc_all = lax.dot_general(jnp.concatenate(q2s, axis=0), kc, dn, preferred_element_type=F32)
    for p in range(npair):
        s = ss[p]
        vb = vbs[p]
        sc = sc_all[p * 2 * nq:(p + 1) * 2 * nq]
        if bounded:
            ov = (jnp.dot(jnp.exp(s).astype(BF16), vb, preferred_element_type=F32)
                  + jnp.dot(jnp.exp(sc).astype(BF16), vc, preferred_element_type=F32))
            o2 = ov[:, 0:LANES] * (1.0 / ov[:, LANES:])
        else:
            m = jnp.maximum(jnp.max(s, axis=-1, keepdims=True), jnp.max(sc, axis=-1, keepdims=True))
            e = jnp.exp(s - m)
            ec = jnp.exp(sc - m)
            den = jnp.sum(e, axis=-1, keepdims=True) + jnp.sum(ec, axis=-1, keepdims=True)
            o2 = (jnp.dot(e.astype(BF16), vb, preferred_element_type=F32)
                  + jnp.dot(ec.astype(BF16), vc, preferred_element_type=F32)) * (1.0 / den)
        o_ref[0, p * nq:(p + 1) * nq, :] = jnp.where(head_lo, o2[:nq], o2[nq:]).astype(BF16)


def _attention(q, k, v, kc, vc, bias_tab, bounded):
    b, s, d = q.shape
    vlanes = 2 * LANES if bounded else LANES
    c = kc.shape[1]
    rows = s // GRID_W
    assert rows % ATT_ROWS == 0 and rows >= ATT_KROWS
    tq = ATT_ROWS * GRID_W
    nt = s // tq
    chunk_rows = 4
    ck = chunk_rows * GRID_W
    nchunk = ATT_KROWS // chunk_rows
    k4 = k.reshape(b, s // ck, ck, d)
    v4 = v.reshape(b, s // ck, ck, d)
    sets = _att_band_layout(None)
    npair = ATT_ROWS // ATT_PAIR
    m0_top = tuple(sets[1 + p][4] for p in range(npair))
    m0_bot = tuple(sets[1 + npair + p][4] for p in range(npair))

    def kmap(t):
        def f(bi, j, i):
            c0 = jnp.clip(i * (ATT_ROWS // chunk_rows) - (WIN_H // 2) // chunk_rows, 0,
                          (rows - ATT_KROWS) // chunk_rows)
            return (bi, c0 + t, 0, j)
        return f

    kv_specs = [pl.BlockSpec((1, 1, ck, LANES), kmap(t)) for t in range(nchunk)]
    return pl.pallas_call(
        functools.partial(_attn_kernel, m0_top=m0_top, m0_bot=m0_bot, bounded=bounded),
        grid=(b, d // LANES, nt),
        in_specs=[pl.BlockSpec((1, tq, LANES), lambda bi, j, i: (bi, i, j))]
                 + kv_specs + kv_specs
                 + [pl.BlockSpec((1, c, LANES), lambda bi, j, i: (bi, 0, j)),
                    pl.BlockSpec((1, c, LANES), lambda bi, j, i: (bi, 0, j)),
                    pl.BlockSpec((1,) + bias_tab.shape[1:], lambda bi, j, i: (j, 0, 0, 0, 0))],
        out_specs=pl.BlockSpec((1, tq, LANES), lambda bi, j, i: (bi, i, j)),
        out_shape=jax.ShapeDtypeStruct((b, s, d), BF16),
        scratch_shapes=[pltpu.VMEM((ATT_KROWS * GRID_W, LANES), BF16),
                        pltpu.VMEM((ATT_KROWS * GRID_W, vlanes), BF16),
                        pltpu.VMEM((c, vlanes), BF16)],
        compiler_params=_cparams(("parallel", "parallel", "arbitrary")),
        name="nbr_attention_bounded" if bounded else "nbr_attention",
    )(q, *([k4] * nchunk), *([v4] * nchunk), kc, vc, bias_tab)


def _ctx_attn_kernel(q_ref, k_ref, v_ref, o_ref):
    lane = lax.broadcasted_iota(jnp.int32, (1, LANES), 1)
    head_lo = lane < (LANES // 2)
    q = q_ref[0]
    k = k_ref[0]
    v = v_ref[0]
    outs = []
    for hh in range(2):
        mask = head_lo if hh == 0 else jnp.logical_not(head_lo)
        qh = jnp.where(mask, q, jnp.zeros_like(q))
        s = lax.dot_general(qh, k, (((1,), (1,)), ((), ())), preferred_element_type=F32)
        m = jnp.max(s, axis=-1, keepdims=True)
        e = jnp.exp(s - m)
        den = jnp.sum(e, axis=-1, keepdims=True)
        o = jnp.dot(e.astype(BF16), v, preferred_element_type=F32)
        outs.append(o * (1.0 / den))
    o_ref[0] = jnp.where(head_lo, outs[0], outs[1]).astype(BF16)


def _ctx_attention(q, k, v):
    b, c, d = q.shape
    spec = pl.BlockSpec((1, c, LANES), lambda bi, j: (bi, 0, j))
    return pl.pallas_call(
        _ctx_attn_kernel,
        grid=(b, d // LANES),
        in_specs=[spec, spec, spec],
        out_specs=spec,
        out_shape=jax.ShapeDtypeStruct((b, c, d), BF16),
        compiler_params=_cparams(("parallel", "parallel")),
        name="ctx_attention",
    )(q, k, v)


def _swiglu_chunks(h, w1_at, w3_at, w2_at, ff, tf):
    acc = None
    for c0 in range(0, ff, tf):
        a = jnp.dot(h, w1_at(c0, tf), preferred_element_type=F32)
        b = jnp.dot(h, w3_at(c0, tf), preferred_element_type=F32)
        t = (a * jax.nn.sigmoid(a) * b).astype(BF16)
        y = jnp.dot(t, w2_at(c0, tf), preferred_element_type=F32)
        acc = y if acc is None else acc + y
    return acc


def _attnproj_ffn_kernel(a_ref, wo_ref, x_ref, gate_a_ref, g_ref, sh_ref, sc_ref, gate_ref,
                         w1_ref, w3_ref, w2_ref, o_ref, *, tf):
    x1 = x_ref[0] + gate_a_ref[0] * jnp.dot(a_ref[0], wo_ref[...], preferred_element_type=F32)
    h = _rms_mod(x1, g_ref[...], sh_ref[0], sc_ref[0]).astype(BF16)
    y = _swiglu_chunks(h, lambda c0, n: w1_ref[:, c0:c0 + n], lambda c0, n: w3_ref[:, c0:c0 + n],
                       lambda c0, n: w2_ref[c0:c0 + n, :], w1_ref.shape[1], tf)
    o_ref[0] = x1 + gate_ref[0] * y


def _attnproj_ffn(a, wo_bf, x, gate_a, norm_g, shift, scale, gate, w1_bf, w3_bf, w2_bf, tm, tf):
    bx, sx, d = x.shape
    ff = w1_bf.shape[1]
    row = lambda b, i: (b, i, 0)
    vec = lambda b, i: (b, 0, 0)
    const2 = lambda b, i: (0, 0)
    once = pl.Buffered(1)
    return pl.pallas_call(
        functools.partial(_attnproj_ffn_kernel, tf=tf),
        grid=(bx, sx // tm),
        in_specs=[pl.BlockSpec((1, tm, d), row),
                  pl.BlockSpec((d, d), const2, pipeline_mode=once),
                  pl.BlockSpec((1, tm, d), row),
                  pl.BlockSpec((1, 1, d), vec),
                  pl.BlockSpec((1, d), const2),
                  pl.BlockSpec((1, 1, d), vec),
                  pl.BlockSpec((1, 1, d), vec),
                  pl.BlockSpec((1, 1, d), vec),
                  pl.BlockSpec((d, ff), const2, pipeline_mode=once),
                  pl.BlockSpec((d, ff), const2, pipeline_mode=once),
                  pl.BlockSpec((ff, d), const2, pipeline_mode=once)],
        out_specs=pl.BlockSpec((1, tm, d), row),
        out_shape=jax.ShapeDtypeStruct(x.shape, F32),
        compiler_params=_cparams(("parallel", "parallel")),
        name="attnproj_ffn",
    )(a, wo_bf, x, gate_a, norm_g.reshape(1, d), shift, scale, gate, w1_bf, w3_bf, w2_bf)


def _inproj_kernel(x_ref, xp_ref, xn_ref, g_ref, sh_ref, sc_ref, w_ref, cw_ref, cb_ref, gb_ref, xc_ref, ext,
                   *, dr):
    i = pl.program_id(1)
    nb = pl.num_programs(1)
    tm = x_ref.shape[1]
    pad = SUBLANES
    xe = jnp.concatenate([xp_ref[0], x_ref[0], xn_ref[0]], axis=0)
    h = _rms_mod(xe, g_ref[...], sh_ref[0], sc_ref[0]).astype(BF16)
    rowi = lax.broadcasted_iota(jnp.int32, (tm + 2 * pad, 1), 0)
    inside = jnp.logical_and(jnp.logical_or(rowi >= pad, i > 0),
                             jnp.logical_or(rowi < tm + pad, i < nb - 1))
    xr = jnp.dot(h, w_ref[:, dr:], preferred_element_type=F32)
    ext[...] = jnp.where(inside, xr, 0.0)
    cw = cw_ref[...]
    xc = cb_ref[...] + sum(
        cw[j:j + 1, :] * ext[pad + j - CONV_LEFT:pad + j - CONV_LEFT + tm, :] for j in range(cw.shape[0]))
    for k in range(dr // LANES):
        xc_ref[0, k] = xc[:, k * LANES:(k + 1) * LANES]
    gb_ref[0] = jnp.dot(h, w_ref[:, :dr], preferred_element_type=F32)[pad:pad + tm]


def _inproj(x, norm_g, shift, scale, w_in_bf, conv_w, conv_b, tm):
    bx, sx, d = x.shape
    dr = w_in_bf.shape[1] // 2
    tb = tm // SUBLANES
    row = lambda b, i: (b, i, 0)
    vec = lambda b, i: (b, 0, 0)
    const2 = lambda b, i: (0, 0)
    out = jax.ShapeDtypeStruct((bx, sx, dr), F32)
    return pl.pallas_call(
        functools.partial(_inproj_kernel, dr=dr),
        grid=(bx, sx // tm),
        in_specs=[pl.BlockSpec((1, tm, d), row),
                  pl.BlockSpec((1, SUBLANES, d), lambda b, i: (b, jnp.maximum(i * tb - 1, 0), 0)),
                  pl.BlockSpec((1, SUBLANES, d),
                               lambda b, i: (b, jnp.minimum((i + 1) * tb, sx // SUBLANES - 1), 0)),
                  pl.BlockSpec((1, d), const2),
                  pl.BlockSpec((1, 1, d), vec),
                  pl.BlockSpec((1, 1, d), vec),
                  pl.BlockSpec((d, 2 * dr), const2),
                  pl.BlockSpec(conv_w.shape, const2),
                  pl.BlockSpec((1, dr), const2)],
        out_specs=[pl.BlockSpec((1, tm, dr), row),
                   pl.BlockSpec((1, dr // LANES, tm, LANES), lambda b, i: (b, 0, i, 0))],
        out_shape=[out, jax.ShapeDtypeStruct((bx, dr // LANES, sx, LANES), F32)],
        scratch_shapes=[pltpu.VMEM((tm + 2 * SUBLANES, dr), F32)],
        compiler_params=_cparams(("parallel", "parallel")),
        name="rglru_inproj",
    )(x, x, x, norm_g.reshape(1, d), shift, scale, w_in_bf, conv_w, conv_b.reshape(1, dr))


def _lru_scan_kernel(xc_hbm, add_hbm, wa_ref, wx_ref, ba_ref, bx_ref, lam_ref, h0_ref, o_hbm, xbuf, abuf, obuf,
                     carry, sem_in, sem_add, sem_out, *, reverse, ngroups, has_add):
    bi = pl.program_id(0)
    i = pl.program_id(1)
    nb = pl.num_programs(1)
    ncg, nl = xbuf.shape[1], xbuf.shape[2]
    t = nl * SUBLANES
    c = ncg * LANES
    slot = i % 2

    def block_start(step):
        return ((nb - 1 - step) if reverse else step) * t

    def in_copies(step, to_slot):
        cps = [pltpu.make_async_copy(xc_hbm.at[bi, :, pl.ds(block_start(step) + ch * nl, nl), :],
                                     xbuf.at[to_slot, :, :, ch, :], sem_in.at[to_slot])
               for ch in range(SUBLANES)]
        if has_add:
            cps += [pltpu.make_async_copy(add_hbm.at[bi, :, pl.ds(block_start(step) + ch * nl, nl), :],
                                          abuf.at[to_slot, :, :, ch, :], sem_add.at[to_slot])
                    for ch in range(SUBLANES)]
        return cps

    def out_copies(step, from_slot):
        return [pltpu.make_async_copy(obuf.at[from_slot, :, :, ch, :],
                                      o_hbm.at[bi, :, pl.ds(block_start(step) + ch * nl, nl), :],
                                      sem_out.at[from_slot])
                for ch in range(SUBLANES)]

    @pl.when(i == 0)
    def _():
        carry[...] = jnp.broadcast_to(h0_ref[0], carry.shape)
        for cp in in_copies(0, 0):
            cp.start()

    @pl.when(i + 1 < nb)
    def _():
        for cp in in_copies(i + 1, 1 - slot):
            cp.start()

    for cp in in_copies(i, slot):
        cp.wait()

    @pl.when(i >= 2)
    def _():
        for cp in out_copies(i, slot):
            cp.wait()

    gw = c // ngroups
    kpg = gw // LANES
    steps = range(nl - 1, -1, -1) if reverse else range(nl)
    rows = lambda v, j: v[j * SUBLANES:(j + 1) * SUBLANES]
    sub = lax.broadcasted_iota(jnp.int32, (SUBLANES, 1), 0)
    for gi in range(ngroups):
        cs = slice(gi * gw, (gi + 1) * gw)
        xc = jnp.concatenate([xbuf[slot, gi * kpg + k].reshape(t, LANES) for k in range(kpg)], axis=1)
        xb = xc.astype(BF16)
        t_a = jnp.tanh(jnp.dot(xb, wa_ref[gi], preferred_element_type=F32) + 0.5 * ba_ref[:, cs])
        t_x = jnp.tanh(jnp.dot(xb, wx_ref[gi], preferred_element_type=F32) + 0.5 * bx_ref[:, cs])
        lam = lam_ref[:, cs]
        sp = jnp.maximum(-lam, 0.0) + jnp.log(1.0 + jnp.exp(-jnp.abs(lam)))
        k1 = ((-0.5 * LRU_C) * np.float32(np.log2(np.e))) * sp
        a = jnp.exp2(t_a * k1 + k1)
        om = 1.0 - a * a
        u = (om * lax.rsqrt(jnp.maximum(om, 1e-37))) * ((0.5 * t_x + 0.5) * xc)

        hloc = jnp.zeros((SUBLANES, gw), F32)
        aprod = jnp.ones((SUBLANES, gw), F32)
        for j in steps:
            hloc = rows(a, j) * hloc + rows(u, j)
            aprod = rows(a, j) * aprod

        hprev = carry[:, cs]
        for sft in (1, 2, 4):
            if reverse:
                a_sh = pltpu.roll(aprod, SUBLANES - sft, axis=0)
                h_sh = pltpu.roll(hloc, SUBLANES - sft, axis=0)
                ok = sub < SUBLANES - sft
            else:
                a_sh = pltpu.roll(aprod, sft, axis=0)
                h_sh = pltpu.roll(hloc, sft, axis=0)
                ok = sub >= sft
            hloc = jnp.where(ok, aprod * h_sh + hloc, hloc)
            aprod = jnp.where(ok, aprod * a_sh, aprod)
        after = hloc + aprod * hprev
        if reverse:
            h = jnp.where(sub == SUBLANES - 1, hprev, pltpu.roll(after, SUBLANES - 1, axis=0))
            carry[:, cs] = jnp.broadcast_to(after[0:1], after.shape)
        else:
            h = jnp.where(sub == 0, hprev, pltpu.roll(after, 1, axis=0))
            carry[:, cs] = jnp.broadcast_to(after[SUBLANES - 1:SUBLANES], after.shape)

        for j in steps:
            h = rows(a, j) * h + rows(u, j)
            for k in range(kpg):
                val = h[:, k * LANES:(k + 1) * LANES]
                if has_add:
                    val = val + abuf[slot, gi * kpg + k, j]
                obuf[slot, gi * kpg + k, j] = val
    for cp in out_copies(i, slot):
        cp.start()

    @pl.when(i == nb - 1)
    def _():
        for cp in out_copies(i, slot):
            cp.wait()

    @pl.when(jnp.logical_and(i == nb - 1, nb >= 2))
    def _():
        for cp in out_copies(i, 1 - slot):
            cp.wait()


def _lru_scan(xc, wa_g, wx_g, ba, bx, lam, h0, reverse, t, add=None):
    b, ncg, s, _ = xc.shape
    c = ncg * LANES
    nb = s // t
    nl = t // SUBLANES
    ngroups = wa_g.shape[0]
    gw = c // ngroups
    has_add = add is not None
    add_buf = (2, ncg, nl, SUBLANES, LANES) if has_add else (1, 1, 1, SUBLANES, LANES)
    const2 = lambda bi, i: (0, 0)
    const3 = lambda bi, i: (0, 0, 0)
    return pl.pallas_call(
        functools.partial(_lru_scan_kernel, reverse=reverse, ngroups=ngroups, has_add=has_add),
        grid=(b, nb),
        in_specs=[pl.BlockSpec(memory_space=pl.ANY),
                  pl.BlockSpec(memory_space=pl.ANY),
                  pl.BlockSpec((ngroups, gw, gw), const3),
                  pl.BlockSpec((ngroups, gw, gw), const3),
                  pl.BlockSpec((1, c), const2),
                  pl.BlockSpec((1, c), const2),
                  pl.BlockSpec((1, c), const2),
                  pl.BlockSpec((1, 1, c), lambda bi, i: (bi, 0, 0))],
        out_specs=pl.BlockSpec(memory_space=pl.ANY),
        out_shape=jax.ShapeDtypeStruct((b, ncg, s, LANES), F32),
        scratch_shapes=[pltpu.VMEM((2, ncg, nl, SUBLANES, LANES), F32),
                        pltpu.VMEM(add_buf, F32),
                        pltpu.VMEM((2, ncg, nl, SUBLANES, LANES), F32),
                        pltpu.VMEM((SUBLANES, c), F32),
                        pltpu.SemaphoreType.DMA((2,)),
                        pltpu.SemaphoreType.DMA((2,)),
                        pltpu.SemaphoreType.DMA((2,))],
        compiler_params=_cparams(("parallel", "arbitrary")),
        name="lru_scan_bwd" if reverse else "lru_scan_fwd",
    )(xc, add if has_add else xc, wa_g, wx_g, ba.reshape(1, c), bx.reshape(1, c), lam.reshape(1, c), h0)


def _lru_out_router_kernel(g_ref, hs_ref, w_ref, x_ref, gate_ref, ng_ref, sh_ref, sc_ref,
                           wh_ref, wl_ref, br_ref, o_ref, hp_ref, rt_ref, *, n_exp):
    hs = jnp.concatenate([hs_ref[0, k] for k in range(hs_ref.shape[1])], axis=1)
    a = (jax.nn.gelu(g_ref[0]) * hs).astype(BF16)
    x1 = x_ref[0] + gate_ref[0] * jnp.dot(a, w_ref[...], preferred_element_type=F32)
    o_ref[0] = x1
    h = _rms_mod(x1, ng_ref[...], sh_ref[0], sc_ref[0])
    hb = h.astype(BF16)
    hf = hb.astype(F32)
    h_lo = (h - hf).astype(BF16)
    logits = (jnp.dot(hb, wh_ref[...], preferred_element_type=F32)
              + jnp.dot(hb, wl_ref[...], preferred_element_type=F32)
              + jnp.dot(h_lo, wh_ref[...], preferred_element_type=F32)) + br_ref[...]
    lane = lax.broadcasted_iota(jnp.int32, logits.shape, 1)
    logits = jnp.where(lane < n_exp, logits, -jnp.inf)
    m1 = jnp.max(logits, axis=-1, keepdims=True)
    i1 = jnp.min(jnp.where(logits == m1, lane, LANES), axis=-1, keepdims=True)
    rest = jnp.where(lane == i1, -jnp.inf, logits)
    m2 = jnp.max(rest, axis=-1, keepdims=True)
    i2 = jnp.min(jnp.where(rest == m2, lane, LANES), axis=-1, keepdims=True)
    e2 = jnp.exp(m2 - m1)
    g1 = 1.0 / (1.0 + e2)
    g2 = e2 * g1
    rt_ref[0] = jnp.where(lane == 0, g1, jnp.where(lane == 1, g2, jnp.where(
        lane == 2, i1.astype(F32), jnp.where(lane == 3, i2.astype(F32), 0.0))))
    hp_ref[0] = h


def _lru_out_router(g, hs, w_out_bf, x, gate, norm_g, shift, scale, w_router, b_router, tm):
    bx, sx, dr = g.shape
    d = x.shape[-1]
    n_exp = w_router.shape[1]
    wpad = jnp.zeros((d, LANES), F32).at[:, :n_exp].set(w_router)
    w_hi = wpad.astype(BF16)
    w_lo = (wpad - w_hi.astype(F32)).astype(BF16)
    bpad = jnp.zeros((1, LANES), F32).at[0, :n_exp].set(b_router)
    row = lambda b, i: (b, i, 0)
    vec = lambda b, i: (b, 0, 0)
    const2 = lambda b, i: (0, 0)
    cg = pl.BlockSpec((1, dr // LANES, tm, LANES), lambda b, i: (b, 0, i, 0))
    return pl.pallas_call(
        functools.partial(_lru_out_router_kernel, n_exp=n_exp),
        grid=(bx, sx // tm),
        in_specs=[pl.BlockSpec((1, tm, dr), row), cg,
                  pl.BlockSpec((dr, d), const2),
                  pl.BlockSpec((1, tm, d), row),
                  pl.BlockSpec((1, 1, d), vec),
                  pl.BlockSpec((1, d), const2),
                  pl.BlockSpec((1, 1, d), vec),
                  pl.BlockSpec((1, 1, d), vec),
                  pl.BlockSpec((d, LANES), const2),
                  pl.BlockSpec((d, LANES), const2),
                  pl.BlockSpec((1, LANES), const2)],
        out_specs=[pl.BlockSpec((1, tm, d), row), pl.BlockSpec((1, tm, d), row),
                   pl.BlockSpec((1, tm, LANES), row)],
        out_shape=[jax.ShapeDtypeStruct(x.shape, F32), jax.ShapeDtypeStruct(x.shape, F32),
                   jax.ShapeDtypeStruct((bx, sx, LANES), F32)],
        compiler_params=_cparams(("parallel", "parallel")),
        name="rglru_outproj_router",
    )(g, hs, w_out_bf, x, gate, norm_g.reshape(1, d), shift, scale, w_hi, w_lo, bpad)


def _dispatch_kernel(dest_ref, fill_ref, hp_ref, xs_ref, zrow, sem, zsem, *, td, n_fill):
    base = pl.program_id(0) * td

    @pl.when(pl.program_id(0) == 0)
    def _():
        zrow[...] = jnp.zeros_like(zrow)

        def zissue(r, _):
            pltpu.make_async_copy(zrow.at[pl.ds(0, 1)], xs_ref.at[pl.ds(fill_ref[r], 1)], zsem).start()
            return 0

        lax.fori_loop(0, n_fill, zissue, 0, unroll=8)

    def issue(r, _):
        for kk in range(TOP_K):
            dst = dest_ref[(base + r) * TOP_K + kk]
            pltpu.make_async_copy(hp_ref.at[pl.ds(r, 1)], xs_ref.at[pl.ds(dst, 1)], sem).start()
        return 0

    lax.fori_loop(0, td, issue, 0, unroll=8)
    for kk in range(TOP_K):
        pltpu.make_async_copy(hp_ref, xs_ref.at[pl.ds(0, td)], sem).wait()

    @pl.when(pl.program_id(0) == 0)
    def _():
        pltpu.make_async_copy(xs_ref.at[pl.ds(0, n_fill)], xs_ref.at[pl.ds(0, n_fill)], zsem).wait()


def _dispatch(hp, dest, fill, n_rows, td):
    n, w = hp.shape
    n_fill = fill.shape[0]
    return pl.pallas_call(
        functools.partial(_dispatch_kernel, td=td, n_fill=n_fill),
        grid_spec=pltpu.PrefetchScalarGridSpec(
            num_scalar_prefetch=2,
            grid=(n // td,),
            in_specs=[pl.BlockSpec((td, w), lambda i, dref, fref: (i, 0))],
            out_specs=pl.BlockSpec(memory_space=pl.ANY),
            scratch_shapes=[pltpu.VMEM((SUBLANES, w), hp.dtype),
                            pltpu.SemaphoreType.DMA(()), pltpu.SemaphoreType.DMA(())]),
        out_shape=jax.ShapeDtypeStruct((n_rows, w), hp.dtype),
        compiler_params=pltpu.CompilerParams(dimension_semantics=("arbitrary",),
                                             vmem_limit_bytes=VMEM_LIMIT, has_side_effects=True),
        name="moe_dispatch",
    )(dest, fill, hp)


def _experts_kernel(te_ref, nu_ref, x_ref, w1_ref, w3_ref, w2_ref, o_ref, *, tf):
    del te_ref
    i = pl.program_id(0)

    @pl.when(i < nu_ref[0])
    def _():
        o_ref[...] = _swiglu_chunks(
            x_ref[...].astype(BF16), lambda c0, n: w1_ref[0, :, c0:c0 + n],
            lambda c0, n: w3_ref[0, :, c0:c0 + n], lambda c0, n: w2_ref[0, c0:c0 + n, :],
            w1_ref.shape[2], tf)

    @pl.when(i >= nu_ref[0])
    def _():
        o_ref[...] = jnp.zeros_like(o_ref)


def _experts(xs, tile_expert, n_used, w1_bf, w3_bf, w2_bf, tm, tf):
    n_rows, w = xs.shape
    n_exp, d, ff = w1_bf.shape
    n_tiles = n_rows // tm

    def live(i, nu):
        return jnp.minimum(i, nu[0] - 1)

    once = pl.Buffered(1)
    return pl.pallas_call(
        functools.partial(_experts_kernel, tf=tf),
        grid_spec=pltpu.PrefetchScalarGridSpec(
            num_scalar_prefetch=2,
            grid=(n_tiles,),
            in_specs=[pl.BlockSpec((tm, w), lambda i, te, nu: (live(i, nu), 0)),
                      pl.BlockSpec((1, d, ff), lambda i, te, nu: (te[live(i, nu)], 0, 0), pipeline_mode=once),
                      pl.BlockSpec((1, d, ff), lambda i, te, nu: (te[live(i, nu)], 0, 0), pipeline_mode=once),
                      pl.BlockSpec((1, ff, d), lambda i, te, nu: (te[live(i, nu)], 0, 0), pipeline_mode=once)],
            out_specs=pl.BlockSpec((tm, d), lambda i, te, nu: (i, 0))),
        out_shape=jax.ShapeDtypeStruct((n_rows, d), F32),
        compiler_params=_cparams(("arbitrary",)),
        name="moe_experts",
    )(tile_expert, n_used, xs, w1_bf, w3_bf, w2_bf)


def _combine_kernel(pos_ref, x_ref, gate_ref, rt_ref, y_ref, o_ref, ybuf, sem, *, tc):
    i = pl.program_id(0)
    slot = i % 2

    def issue(tile, to_slot):
        base = tile * tc

        def body(r, _):
            for kk in range(TOP_K):
                src = pos_ref[(base + r) * TOP_K + kk]
                pltpu.make_async_copy(y_ref.at[pl.ds(src, 1)], ybuf.at[to_slot, kk, pl.ds(r, 1)],
                                      sem.at[to_slot]).start()
            return 0

        lax.fori_loop(0, tc, body, 0, unroll=8)

    @pl.when(i == 0)
    def _():
        issue(0, 0)

    @pl.when(i + 1 < pl.num_programs(0))
    def _():
        issue(i + 1, 1 - slot)

    for kk in range(TOP_K):
        pltpu.make_async_copy(y_ref.at[pl.ds(0, tc)], ybuf.at[slot, kk], sem.at[slot]).wait()
    rt = rt_ref[...]
    y = rt[:, 0:1] * ybuf[slot, 0] + rt[:, 1:2] * ybuf[slot, 1]
    o_ref[...] = x_ref[...] + gate_ref[0] * y


def _combine(x2, gate, route, y, pos, seq, tc):
    n, d = x2.shape
    return pl.pallas_call(
        functools.partial(_combine_kernel, tc=tc),
        grid_spec=pltpu.PrefetchScalarGridSpec(
            num_scalar_prefetch=1,
            grid=(n // tc,),
            in_specs=[pl.BlockSpec((tc, d), lambda i, p: (i, 0)),
                      pl.BlockSpec((1, 1, d), lambda i, p: ((i * tc) // seq, 0, 0)),
                      pl.BlockSpec((tc, LANES), lambda i, p: (i, 0)),
                      pl.BlockSpec(memory_space=pl.ANY)],
            out_specs=pl.BlockSpec((tc, d), lambda i, p: (i, 0)),
            scratch_shapes=[pltpu.VMEM((2, TOP_K, tc, d), F32), pltpu.SemaphoreType.DMA((2,))]),
        out_shape=jax.ShapeDtypeStruct((n, d), F32),
        compiler_params=_cparams(("arbitrary",)),
        name="moe_combine",
    )(pos, x2, gate, route, y)


def _route_positions(e_flat, n_exp, tm):
    a = e_flat.shape[0]
    onehot = (e_flat[:, None] == jnp.arange(n_exp, dtype=jnp.int32)[None, :]).astype(jnp.int32)
    csum = jnp.cumsum(onehot, axis=0)
    counts = csum[-1]
    padded = ((counts + tm - 1) // tm) * tm
    ends = jnp.cumsum(padded)
    starts = ends - padded
    dest = jnp.sum(onehot * (csum - 1 + starts[None, :]), axis=1).astype(jnp.int32)
    n_tiles = a // tm + n_exp
    tile_ids = jnp.arange(n_tiles, dtype=jnp.int32)
    tile_expert = jnp.minimum(jnp.sum((ends[None, :] // tm <= tile_ids[:, None]).astype(jnp.int32), axis=1),
                              n_exp - 1)
    n_used = (ends[-1] // tm).astype(jnp.int32).reshape(1)
    j = jnp.arange(tm, dtype=jnp.int32)[None, :]
    is_pad = counts[:, None] + j < padded[:, None]
    tail_rank = jnp.cumsum((~is_pad).astype(jnp.int32).reshape(-1)) - 1
    fill = jnp.where(is_pad.reshape(-1), (starts[:, None] + counts[:, None] + j).reshape(-1),
                     ends[-1] + tail_rank).astype(jnp.int32)
    return dest, fill, tile_expert.astype(jnp.int32), n_used, n_tiles * tm


def _blockdiag_groups(w, ngroups):
    nblk, bw, _ = w.shape
    per = nblk // ngroups
    eye = jnp.eye(per, dtype=w.dtype)
    wg = w.reshape(ngroups, per, bw, bw)
    full = wg[:, :, :, None, :] * eye[None, :, None, :, None]
    return full.reshape(ngroups, per * bw, per * bw)


def kernel(x, c, ctx, c_ctx, l0_w_mod, l0_b_mod, l0_norm1, l0_norm2, l0_w_qkv, l0_q_gain, l0_k_gain, l0_rpb, l0_w_o, l0_ffn_w1, l0_ffn_w3, l0_ffn_w2, l1_w_mod, l1_b_mod, l1_norm1, l1_norm2, l1_w_in, l1_conv_w, l1_conv_b, l1_gate_a_w, l1_gate_a_b, l1_gate_x_w, l1_gate_x_b, l1_lam, l1_w_out, l1_router_w, l1_router_b, l1_moe_w1, l1_moe_w3, l1_moe_w2):
    b, s, d = x.shape
    n_ctx = ctx.shape[1]
    head_dim = l0_q_gain.shape[0]
    n_heads = d // head_dim
    n_exp = l1_router_w.shape[1]
    tm = min(512, s)
    tmc = min(512, n_ctx)

    def mods(w_mod, b_mod):
        m = _adaln_mod(jnp.concatenate([c, c_ctx[None, :]], axis=0), w_mod, b_mod)
        lat = [m[:b, j * d:(j + 1) * d].reshape(b, 1, d) for j in range(6)]
        cx = [jnp.broadcast_to(m[b:b + 1, j * d:(j + 1) * d].reshape(1, 1, d), (b, 1, d)) for j in range(6)]
        return lat, cx

    ml, mc = mods(l0_w_mod, l0_b_mod)
    w_qkv = l0_w_qkv.astype(BF16)
    qg = (jnp.tile(l0_q_gain, n_heads) * (head_dim ** -0.5)).reshape(1, d)
    kg = jnp.tile(l0_k_gain, n_heads).reshape(1, d)
    ql, kl, vl = _qkv(x, l0_norm1, ml[0], ml[1], w_qkv, qg, kg, head_dim, tm)
    qc, kc, vc = _qkv(ctx, l0_norm1, mc[0], mc[1], w_qkv, qg, kg, head_dim, tmc)
    logit_bound = (1.02 * head_dim) * jnp.max(jnp.abs(qg)) * jnp.max(jnp.abs(kg)) + jnp.max(jnp.abs(l0_rpb))
    bias_tab = _att_bias_table(l0_rpb)
    o_l = lax.cond(logit_bound <= ATT_LOGIT_BOUND,
                   lambda: _attention(ql, kl, vl, kc, vc, bias_tab, True),
                   lambda: _attention(ql, kl, vl, kc, vc, bias_tab, False))
    o_c = _ctx_attention(qc, kc, vc)
    w_o = l0_w_o.astype(BF16)
    w1, w3, w2 = l0_ffn_w1.astype(BF16), l0_ffn_w3.astype(BF16), l0_ffn_w2.astype(BF16)
    tf0 = l0_ffn_w1.shape[1]
    x_l = _attnproj_ffn(o_l, w_o, x, ml[2], l0_norm2, ml[3], ml[4], ml[5], w1, w3, w2, tm, tf0)
    x_c = _attnproj_ffn(o_c, w_o, ctx, mc[2], l0_norm2, mc[3], mc[4], mc[5], w1, w3, w2, tmc, tf0)

    ml, mc = mods(l1_w_mod, l1_b_mod)
    w_in = l1_w_in.astype(BF16)
    g_l, xr_l = _inproj(x_l, l1_norm1, ml[0], ml[1], w_in, l1_conv_w, l1_conv_b, tm)
    _, xr_c = _inproj(x_c, l1_norm1, mc[0], mc[1], w_in, l1_conv_w, l1_conv_b, tmc)
    dr = g_l.shape[-1]
    ngroups = 4
    t_scan = min(256, n_ctx)
    zeros0 = jnp.zeros((b, 1, dr), F32)
    hs = None
    for di, rev in enumerate((False, True)):
        wa_g = (0.5 * _blockdiag_groups(l1_gate_a_w[di], ngroups)).astype(BF16)
        wx_g = (0.5 * _blockdiag_groups(l1_gate_x_w[di], ngroups)).astype(BF16)
        args = (wa_g, wx_g, l1_gate_a_b[di], l1_gate_x_b[di], l1_lam[di])
        h_c = _lru_scan(xr_c, *args, zeros0, rev, t_scan)
        h0 = (h_c[:, :, 0, :] if rev else h_c[:, :, n_ctx - 1, :]).reshape(b, 1, dr)
        hs = _lru_scan(xr_l, *args, h0, rev, t_scan, add=hs)
    x_l, hp, route = _lru_out_router(g_l, hs, l1_w_out.astype(BF16), x_l, ml[2],
                                     l1_norm2, ml[3], ml[4], l1_router_w, l1_router_b, tm)
    n = b * s
    route2 = route.reshape(n, LANES)
    e_flat = route2[:, 2:2 + TOP_K].astype(jnp.int32).reshape(n * TOP_K)
    tme = 512
    dest, fill, tile_expert, n_used, n_rows = _route_positions(e_flat, n_exp, tme)
    xs = _dispatch(hp.reshape(n, d), dest, fill, n_rows, min(512, n))
    y = _experts(xs, tile_expert, n_used, l1_moe_w1.astype(BF16), l1_moe_w3.astype(BF16),
                 l1_moe_w2.astype(BF16), tme, l1_moe_w1.shape[2] // 2)
    out = _combine(x_l.reshape(n, d), ml[5], route2, y, dest, s, min(256, n))
    return out.reshape(b, s, d)
```
